```python
import jax
import jax.numpy as jnp
from jax import lax
import numpy as np

D_MODEL = 1024
BATCH = 8
SEQ = 4096
DEPTH = 4

GRID_W = 64
CTX_LEN = 256
N_MIXERS = 3
CHUNK = 128
SG_GROUPS = 8
SG_HALF = 2 * D_MODEL
NA_HEADS = 16
NA_HEAD_DIM = D_MODEL // NA_HEADS
NA_KH_MAX = 8
NA_KW = 16
CONV_WIDTH = 31
D_FF = 4 * D_MODEL
EPS = 1e-6
N_A = len(range(0, DEPTH, N_MIXERS))
N_B = len(range(1, DEPTH, N_MIXERS))
N_C = len(range(2, DEPTH, N_MIXERS))
LAST_CTX_READER = max(range(1, DEPTH, N_MIXERS))

kernel_name = 'hybrid_sgu_natten_conformer_dit'


def rms_norm(x, g):
    xf = x.astype(jnp.float32)
    y = xf * lax.rsqrt(jnp.mean(xf * xf, axis=-1, keepdims=True) + EPS)
    return (y * g.astype(jnp.float32)).astype(x.dtype)


def layer_norm(x, g, b):
    xf = x.astype(jnp.float32)
    xc = xf - jnp.mean(xf, axis=-1, keepdims=True)
    y = xc * lax.rsqrt(jnp.mean(xc * xc, axis=-1, keepdims=True) + EPS)
    return (y * g.astype(jnp.float32) + b.astype(jnp.float32)).astype(x.dtype)


def ada_mod(cond, w, b):
    return jnp.split(jax.nn.silu(cond) @ w + b, 6, axis=-1)


def modulate(h, shift, scale):
    return h * (1 + scale) + shift


def chunk_sgu(h, w_in, b_in, ln_g, ln_b, w_s, b_s, w_out):
    bsz, n, _ = h.shape
    u, v = jnp.split(jax.nn.gelu(h @ w_in + b_in), 2, axis=-1)
    v = layer_norm(v, ln_g, ln_b)
    v = v.reshape(bsz, n // CHUNK, CHUNK, SG_GROUPS, SG_HALF // SG_GROUPS)
    v = jnp.einsum('gpq,bnqgc->bnpgc', w_s, v) + b_s.T[:, :, None]
    return (u * v.reshape(bsz, n, SG_HALF)) @ w_out


def na_attention(h, hc, w_qkv, w_o, q_g, k_g, rpb):
    bsz, n, _ = h.shape
    rows = n // GRID_W
    kh = min(NA_KH_MAX, rows)

    def heads(t):
        return t.reshape(t.shape[0], t.shape[1], NA_HEADS, NA_HEAD_DIM)

    def grid(t):
        return t.reshape(bsz, rows, GRID_W, NA_HEADS, NA_HEAD_DIM).transpose(0, 3, 1, 2, 4)

    q, k, v = jnp.split(h @ w_qkv, 3, axis=-1)
    kc, vc = jnp.split(hc @ w_qkv[:, D_MODEL:], 2, axis=-1)
    q = grid(rms_norm(heads(q), q_g) * (NA_HEAD_DIM ** -0.5))
    k = grid(rms_norm(heads(k), k_g))
    v = grid(heads(v))
    kc = rms_norm(heads(kc), k_g).transpose(0, 2, 1, 3)
    vc = heads(vc).transpose(0, 2, 1, 3)

    cols = np.arange(GRID_W)
    col_idx = np.clip(cols - NA_KW // 2, 0, GRID_W - NA_KW)[:, None] + np.arange(NA_KW)[None, :]
    col_off = col_idx - cols[:, None] + (NA_KW - 1)
    n_loc = kh * NA_KW

    def row_block(r):
        rs = jnp.clip(r - kh // 2, 0, rows - kh)
        q_r = lax.dynamic_index_in_dim(q, r, axis=2, keepdims=False)
        k_win = lax.dynamic_slice_in_dim(k, rs, kh, axis=2)[:, :, :, col_idx]
        v_win = lax.dynamic_slice_in_dim(v, rs, kh, axis=2)[:, :, :, col_idx]
        row_off = rs + jnp.arange(kh) - r + (NA_KH_MAX - 1)
        bias = rpb[:, row_off[:, None, None], col_off[None]].transpose(0, 2, 1, 3)
        s_loc = jnp.einsum('bhqd,bhiqjd->bhqij', q_r, k_win) + bias
        s_ctx = jnp.einsum('bhqd,bhcd->bhqc', q_r, kc)
        s = jnp.concatenate([s_loc.reshape(bsz, NA_HEADS, GRID_W, n_loc), s_ctx], axis=-1)
        p = jax.nn.softmax(s.astype(jnp.float32), axis=-1).astype(v.dtype)
        p_loc = p[..., :n_loc].reshape(bsz, NA_HEADS, GRID_W, kh, NA_KW)
        return (jnp.einsum('bhqij,bhiqjd->bhqd', p_loc, v_win)
                + jnp.einsum('bhqc,bhcd->bhqd', p[..., n_loc:], vc))

    o = lax.map(row_block, jnp.arange(rows))
    o = o.transpose(1, 0, 3, 2, 4).reshape(bsz, n, D_MODEL)
    return o @ w_o


def conformer_conv(h, w_pw1, b_pw1, w_dw, b_dw, ln_g, ln_b, w_pw2, b_pw2):
    a = jax.nn.glu(h @ w_pw1 + b_pw1, axis=-1)
    a = lax.conv_general_dilated(a, w_dw[:, None, :], window_strides=(1,),
                                 padding=[(CONV_WIDTH // 2, CONV_WIDTH // 2)],
                                 dimension_numbers=('NWC', 'WIO', 'NWC'),
                                 feature_group_count=D_MODEL) + b_dw
    a = jax.nn.silu(layer_norm(a, ln_g, ln_b))
    return a @ w_pw2 + b_pw2


def sq_relu_mlp(h, w1, w2):
    return jnp.square(jax.nn.relu(h @ w1)) @ w2


def setup_inputs(seed: int = 0) -> dict:
    key = jax.random.key(seed)
    ks = iter(jax.random.split(key, 32))

    def nrm(shape, scale):
        return jax.random.normal(next(ks), shape, jnp.float32) * scale

    def gain(shape):
        return 1.0 + nrm(shape, 0.02)

    d = D_MODEL
    return {
        'x': nrm((BATCH, SEQ, d), 1.0),
        'c': nrm((BATCH, d), 1.0),
        'ctx': nrm((BATCH, CTX_LEN, d), 1.0),
        'c_ctx': nrm((d,), 1.0),
        'ada_w': nrm((DEPTH, d, 6 * d), 0.5 * d ** -0.5),
        'ada_b': nrm((DEPTH, 6 * d), 0.02),
        'norm1_g': gain((DEPTH, d)),
        'norm2_g': gain((DEPTH, d)),
        'a_w_in': nrm((N_A, d, 2 * SG_HALF), d ** -0.5),
        'a_b_in': nrm((N_A, 2 * SG_HALF), 0.02),
        'a_ln_g': gain((N_A, SG_HALF)),
        'a_ln_b': nrm((N_A, SG_HALF), 0.02),
        'a_w_s': nrm((N_A, SG_GROUPS, CHUNK, CHUNK), CHUNK ** -0.5),
        'a_b_s': gain((N_A, SG_GROUPS, CHUNK)),
        'a_w_out': nrm((N_A, SG_HALF, d), SG_HALF ** -0.5),
        'b_w_qkv': nrm((N_B, d, 3 * d), d ** -0.5),
        'b_w_o': nrm((N_B, d, d), d ** -0.5),
        'b_q_g': gain((N_B, NA_HEAD_DIM)),
        'b_k_g': gain((N_B, NA_HEAD_DIM)),
        'b_rpb': nrm((N_B, NA_HEADS, 2 * NA_KH_MAX - 1, 2 * NA_KW - 1), 0.3),
        'c_w_pw1': nrm((N_C, d, 2 * d), d ** -0.5),
        'c_b_pw1': nrm((N_C, 2 * d), 0.02),
        'c_w_dw': nrm((N_C, CONV_WIDTH, d), CONV_WIDTH ** -0.5),
        'c_b_dw': nrm((N_C, d), 0.02),
        'c_ln_g': gain((N_C, d)),
        'c_ln_b': nrm((N_C, d), 0.02),
        'c_w_pw2': nrm((N_C, d, d), d ** -0.5),
        'c_b_pw2': nrm((N_C, d), 0.02),
        'mlp_w1': nrm((DEPTH, d, D_FF), d ** -0.5),
        'mlp_w2': nrm((DEPTH, D_FF, d), D_FF ** -0.5),
    }


def reference(x, c, ctx, c_ctx, ada_w, ada_b, norm1_g, norm2_g,
              a_w_in, a_b_in, a_ln_g, a_ln_b, a_w_s, a_b_s, a_w_out,
              b_w_qkv, b_w_o, b_q_g, b_k_g, b_rpb,
              c_w_pw1, c_b_pw1, c_w_dw, c_b_dw, c_ln_g, c_ln_b, c_w_pw2, c_b_pw2,
              mlp_w1, mlp_w2):
    h, hc = x, ctx
    for l in range(DEPTH):
        kind, j = l % N_MIXERS, l // N_MIXERS
        sh1, sc1, g1, sh2, sc2, g2 = (m[:, None, :] for m in ada_mod(c, ada_w[l], ada_b[l]))
        hn = modulate(rms_norm(h, norm1_g[l]), sh1, sc1)
        if l <= LAST_CTX_READER:
            csh1, csc1, cg1, csh2, csc2, cg2 = ada_mod(c_ctx, ada_w[l], ada_b[l])
            cn = modulate(rms_norm(hc, norm1_g[l]), csh1, csc1)
        if kind == 1:
            mix = na_attention(hn, cn, b_w_qkv[j], b_w_o[j], b_q_g[j], b_k_g[j], b_rpb[j])
        else:
            if kind == 0:
                mixer = chunk_sgu
                p = (a_w_in[j], a_b_in[j], a_ln_g[j], a_ln_b[j], a_w_s[j], a_b_s[j], a_w_out[j])
            else:
                mixer = conformer_conv
                p = (c_w_pw1[j], c_b_pw1[j], c_w_dw[j], c_b_dw[j], c_ln_g[j], c_ln_b[j], c_w_pw2[j], c_b_pw2[j])
            mix = mixer(hn, *p)
            if l < LAST_CTX_READER:
                hc = hc + cg1 * mixer(cn, *p)
                hc = hc + cg2 * sq_relu_mlp(modulate(rms_norm(hc, norm2_g[l]), csh2, csc2), mlp_w1[l], mlp_w2[l])
        h = h + g1 * mix
        h = h + g2 * sq_relu_mlp(modulate(rms_norm(h, norm2_g[l]), sh2, sc2), mlp_w1[l], mlp_w2[l])
    return h
```

```python
import functools

import jax
import jax.numpy as jnp
import numpy as np
from jax.experimental import pallas as pl
from jax.experimental.pallas import tpu as pltpu

D = 1024
DEPTH = 4
GRID_W = 64
N_MIXERS = 3
CHUNK = 128
SG_GROUPS = 8
SG_HALF = 2 * D
SG_GW = SG_HALF // SG_GROUPS
NA_HEADS = 16
NA_DH = D // NA_HEADS
NA_KH = 8
NA_KW = 16
CONV_W = 31
CONV_PAD = CONV_W // 2
D_FF = 4 * D
EPS = 1e-6
LAST_CTX_READER = max(range(1, DEPTH, N_MIXERS))

MOD_ROWS = 16
CTX_ROW = 8
NEG = -1e30
VMEM_LIMIT = 56 * 1024 * 1024

NA_R = 4
NA_BAND = 12
HALO = 16


def _cparams(sem):
    return pltpu.CompilerParams(dimension_semantics=sem, vmem_limit_bytes=VMEM_LIMIT)


def _norm_mod(x, g, sh, sc):
    ms = jnp.mean(x * x, axis=-1, keepdims=True)
    return (x * jax.lax.rsqrt(ms + EPS) * g) * (1.0 + sc) + sh


def _layer_norm(x, g, b):
    mu = jnp.mean(x, axis=-1, keepdims=True)
    xc = x - mu
    var = jnp.mean(xc * xc, axis=-1, keepdims=True)
    return xc * jax.lax.rsqrt(var + EPS) * g + b


def _dot(a, b):
    return jnp.dot(a, b, preferred_element_type=jnp.float32)


def _ada_kernel(c_ref, w_ref, b_ref, o_ref):
    cond = c_ref[...]
    act = (cond * jax.nn.sigmoid(cond)).astype(jnp.bfloat16)
    o_ref[0] = _dot(act, w_ref[0].astype(jnp.bfloat16)) + b_ref[0]


def _ada_mods(cond, ada_w, ada_b):
    tn = 1536
    out = pl.pallas_call(
        _ada_kernel,
        out_shape=jax.ShapeDtypeStruct((DEPTH, MOD_ROWS, 6 * D), jnp.float32),
        grid=(DEPTH, 6 * D // tn),
        in_specs=[
            pl.BlockSpec((MOD_ROWS, D), lambda l, n: (0, 0)),
            pl.BlockSpec((1, D, tn), lambda l, n: (l, 0, n)),
            pl.BlockSpec((1, 1, tn), lambda l, n: (l, 0, n)),
        ],
        out_specs=pl.BlockSpec((1, MOD_ROWS, tn), lambda l, n: (l, 0, n)),
        compiler_params=_cparams(("arbitrary", "arbitrary")),
        name="ada_mods",
    )(cond, ada_w, ada_b.reshape(DEPTH, 1, 6 * D))
    return out.reshape(DEPTH * MOD_ROWS, 1, 6 * D)


def _mod_spec(layer, chunk, tm, rows_per_batch):
    if rows_per_batch is None:
        return pl.BlockSpec((1, 1, D), lambda i, *_: (layer * MOD_ROWS + CTX_ROW, 0, chunk))
    return pl.BlockSpec(
        (1, 1, D), lambda i, *_: (layer * MOD_ROWS + (i * tm) // rows_per_batch, 0, chunk))


def _mlp_kernel(x_ref, g_ref, sh_ref, sc_ref, gate_ref, w1_ref, w2_ref, o_ref, hn_ref, acc_ref):
    f = pl.program_id(1)

    @pl.when(f == 0)
    def _():
        hn_ref[...] = _norm_mod(x_ref[...], g_ref[0], sh_ref[0], sc_ref[0]).astype(jnp.bfloat16)

    a = jnp.maximum(_dot(hn_ref[...], w1_ref[...]), 0.0)
    part = _dot((a * a).astype(jnp.bfloat16), w2_ref[...])

    @pl.when(f == 0)
    def _():
        acc_ref[...] = part

    @pl.when(f > 0)
    def _():
        acc_ref[...] += part

    @pl.when(f == pl.num_programs(1) - 1)
    def _():
        o_ref[...] = x_ref[...] + gate_ref[0] * acc_ref[...]


def _mlp(h, mods, norm_g, w1, w2, layer, rows_per_batch, tm, tf=1024):
    m = h.shape[0]
    ms = functools.partial(_mod_spec, layer, tm=tm, rows_per_batch=rows_per_batch)
    return pl.pallas_call(
        _mlp_kernel,
        out_shape=jax.ShapeDtypeStruct((m, D), jnp.float32),
        grid=(m // tm, D_FF // tf),
        in_specs=[
            pl.BlockSpec((tm, D), lambda i, f: (i, 0)),
            pl.BlockSpec((1, 1, D), lambda i, f: (layer, 0, 0)),
            ms(3), ms(4), ms(5),
            pl.BlockSpec((D, tf), lambda i, f: (0, f)),
            pl.BlockSpec((tf, D), lambda i, f: (f, 0)),
        ],
        out_specs=pl.BlockSpec((tm, D), lambda i, f: (i, 0)),
        scratch_shapes=[pltpu.VMEM((tm, D), jnp.bfloat16), pltpu.VMEM((tm, D), jnp.float32)],
        compiler_params=_cparams(("parallel", "arbitrary")),
        name="sq_relu_mlp",
    )(h, norm_g, mods, mods, mods, w1, w2)


def _sgu_in_kernel(x_ref, g_ref, sh_ref, sc_ref, w_ref, b_ref, lg_ref, lb_ref, o_ref, hn_ref):
    n = pl.program_id(1)

    @pl.when(n == 0)
    def _():
        hn_ref[...] = _norm_mod(x_ref[...], g_ref[0], sh_ref[0], sc_ref[0]).astype(jnp.bfloat16)

    y = jax.nn.gelu(_dot(hn_ref[...], w_ref[...]) + b_ref[...])

    @pl.when(n == 0)
    def _():
        o_ref[...] = y.astype(jnp.bfloat16)

    @pl.when(n == 1)
    def _():
        o_ref[...] = _layer_norm(y, lg_ref[...], lb_ref[...]).astype(jnp.bfloat16)


def _sgu_in(h, mods, norm_g, w_in, b_in, ln_g, ln_b, layer, rows_per_batch, tm):
    m = h.shape[0]
    ms = functools.partial(_mod_spec, layer, tm=tm, rows_per_batch=rows_per_batch)
    return pl.pallas_call(
        _sgu_in_kernel,
        out_shape=jax.ShapeDtypeStruct((m, 2 * SG_HALF), jnp.bfloat16),
        grid=(m // tm, 2),
        in_specs=[
            pl.BlockSpec((tm, D), lambda i, n: (i, 0)),
            pl.BlockSpec((1, 1, D), lambda i, n: (layer, 0, 0)),
            ms(0), ms(1),
            pl.BlockSpec((D, SG_HALF), lambda i, n: (0, n)),
            pl.BlockSpec((1, SG_HALF), lambda i, n: (0, n)),
            pl.BlockSpec((1, SG_HALF), lambda i, n: (0, 0)),
            pl.BlockSpec((1, SG_HALF), lambda i, n: (0, 0)),
        ],
        out_specs=pl.BlockSpec((tm, SG_HALF), lambda i, n: (i, n)),
        scratch_shapes=[pltpu.VMEM((tm, D), jnp.bfloat16)],
        compiler_params=_cparams(("parallel", "arbitrary")),
        name="sgu_in",
    )(h, norm_g, mods, mods, w_in, b_in, ln_g, ln_b)


def _sgu_out_kernel(x_ref, gate_ref, u_ref, v_ref, ws_ref, bs_ref, wo_ref, o_ref, gated_ref):
    tm = x_ref.shape[0]
    for ci in range(tm // CHUNK):
        rows = slice(ci * CHUNK, (ci + 1) * CHUNK)
        for g in range(SG_GROUPS):
            cols = slice(g * SG_GW, (g + 1) * SG_GW)
            vs = _dot(ws_ref[g], v_ref[rows, cols]) + bs_ref[:, cols]
            gated_ref[rows, cols] = (u_ref[rows, cols].astype(jnp.float32) * vs).astype(jnp.bfloat16)
    o_ref[...] = x_ref[...] + gate_ref[0] * _dot(gated_ref[...], wo_ref[...])


def _sgu_out(h, mods, uv, w_s, bs_full, w_out, layer, rows_per_batch, tm):
    m = h.shape[0]
    ms = functools.partial(_mod_spec, layer, tm=tm, rows_per_batch=rows_per_batch)
    return pl.pallas_call(
        _sgu_out_kernel,
        out_shape=jax.ShapeDtypeStruct((m, D), jnp.float32),
        grid=(m // tm,),
        in_specs=[
            pl.BlockSpec((tm, D), lambda i: (i, 0)),
            ms(2),
            pl.BlockSpec((tm, SG_HALF), lambda i: (i, 0)),
            pl.BlockSpec((tm, SG_HALF), lambda i: (i, 1)),
            pl.BlockSpec((SG_GROUPS, CHUNK, CHUNK), lambda i: (0, 0, 0)),
            pl.BlockSpec((CHUNK, SG_HALF), lambda i: (0, 0)),
            pl.BlockSpec((SG_HALF, D), lambda i: (0, 0)),
        ],
        out_specs=pl.BlockSpec((tm, D), lambda i: (i, 0)),
        scratch_shapes=[pltpu.VMEM((tm, SG_HALF), jnp.bfloat16)],
        compiler_params=_cparams(("parallel",)),
        name="sgu_out",
    )(h, mods, uv, uv, w_s, bs_full, w_out)


def _qkv_kernel(x_ref, g_ref, sh_ref, sc_ref, w_ref, hg_ref, o_ref, hn_ref, *, n_first):
    n = pl.program_id(1)

    @pl.when(n == 0)
    def _():
        hn_ref[...] = _norm_mod(x_ref[...], g_ref[0], sh_ref[0], sc_ref[0]).astype(jnp.bfloat16)

    y = _dot(hn_ref[...], w_ref[...])

    @pl.when(n + n_first < 2)
    def _():
        lane = jax.lax.broadcasted_iota(jnp.int32, (1, 2 * NA_DH), 1)
        first = lane < NA_DH
        for j in range(D // (2 * NA_DH)):
            cols = slice(j * 2 * NA_DH, (j + 1) * 2 * NA_DH)
            t = y[:, cols]
            t2 = t * t
            s_all = jnp.sum(t2, axis=-1, keepdims=True)
            s_first = jnp.sum(jnp.where(first, t2, 0.0), axis=-1, keepdims=True)
            ms = jnp.where(first, s_first, s_all - s_first) * (1.0 / NA_DH)
            o_ref[:, cols] = (t * jax.lax.rsqrt(ms + EPS) * hg_ref[0, :, cols]).astype(jnp.bfloat16)

    @pl.when(n + n_first >= 2)
    def _():
        o_ref[...] = y.astype(jnp.bfloat16)


def _qkv(h, mods, norm_g, w_qkv, head_gains, layer, rows_per_batch, tm, n_first):
    m = h.shape[0]
    nn = 3 - n_first
    ms = functools.partial(_mod_spec, layer, tm=tm, rows_per_batch=rows_per_batch)
    return pl.pallas_call(
        functools.partial(_qkv_kernel, n_first=n_first),
        out_shape=jax.ShapeDtypeStruct((m, nn * D), jnp.bfloat16),
        grid=(m // tm, nn),
        in_specs=[
            pl.BlockSpec((tm, D), lambda i, n: (i, 0)),
            pl.BlockSpec((1, 1, D), lambda i, n: (layer, 0, 0)),
            ms(0), ms(1),
            pl.BlockSpec((D, D), lambda i, n: (0, n + n_first)),
            pl.BlockSpec((1, 1, D), lambda i, n: (n + n_first, 0, 0)),
        ],
        out_specs=pl.BlockSpec((tm, D), lambda i, n: (i, n)),
        scratch_shapes=[pltpu.VMEM((tm, D), jnp.bfloat16)],
        compiler_params=_cparams(("parallel", "arbitrary")),
        name="na_qkv",
    )(h, norm_g, mods, mods, w_qkv, head_gains)


def _na_kernel(q_ref, k_ref, v_ref, kc_ref, vc_ref, bl_ref, br_ref, o_ref):
    rb = pl.program_id(2)
    rows = k_ref.shape[1] // GRID_W
    r0 = rb * NA_R
    kb = jnp.clip(r0 - NA_KH // 2, 0, rows - NA_BAND)
    start = pl.multiple_of(kb * GRID_W, GRID_W)
    kband = k_ref[0, pl.ds(start, NA_BAND * GRID_W), :]
    vband = v_ref[0, pl.ds(start, NA_BAND * GRID_W), :]
    kc = kc_ref[0]
    vc = vc_ref[0]
    q = q_ref[0]
    lane = jax.lax.broadcasted_iota(jnp.int32, (1, 2 * NA_DH), 1)
    first = lane < NA_DH
    nt = (((1,), (1,)), ((), ()))
    outs = []
    for hh in range(2):
        qm = jnp.where(first if hh == 0 else jnp.logical_not(first), q, jnp.zeros_like(q))
        s_loc = jax.lax.dot_general(qm, kband, nt, preferred_element_type=jnp.float32)
        s_ctx = jax.lax.dot_general(qm, kc, nt, preferred_element_type=jnp.float32)
        bias_rows = []
        for i in range(NA_R):
            r = r0 + i
            rs = jnp.clip(r - NA_KH // 2, 0, rows - NA_KH)
            pieces = []
            for jp in range(NA_BAND // 2):
                idx = []
                for j in (2 * jp, 2 * jp + 1):
                    kr = kb + j
                    valid = jnp.logical_and(kr >= rs, kr < rs + NA_KH)
                    idx.append(jnp.where(valid, kr - r + (NA_KH - 1), 2 * NA_KH - 1))
                pieces.append(bl_ref[hh, idx[0]] + br_ref[hh, idx[1]])
            bias_rows.append(jnp.concatenate(pieces, axis=1))
        s_loc = s_loc + jnp.concatenate(bias_rows, axis=0)
        mx = jnp.maximum(jnp.max(s_loc, axis=-1, keepdims=True), jnp.max(s_ctx, axis=-1, keepdims=True))
        p_loc = jnp.exp(s_loc - mx)
        p_ctx = jnp.exp(s_ctx - mx)
        den = jnp.sum(p_loc, axis=-1, keepdims=True) + jnp.sum(p_ctx, axis=-1, keepdims=True)
        o = _dot(p_loc.astype(jnp.bfloat16), vband) + _dot(p_ctx.astype(jnp.bfloat16), vc)
        outs.append(o / den)
    o_ref[0] = jnp.where(first, outs[0], outs[1]).astype(jnp.bfloat16)


def _na_bias_tables(rpb):
    cols = np.arange(GRID_W)
    cs = np.clip(cols - NA_KW // 2, 0, GRID_W - NA_KW)
    rel = cols[None, :] - cols[:, None] + (NA_KW - 1)
    ok = (cols[None, :] >= cs[:, None]) & (cols[None, :] < cs[:, None] + NA_KW)
    dense = jnp.where(ok[None, None], rpb[:, :, np.clip(rel, 0, 2 * NA_KW - 2)], NEG)
    dense = jnp.concatenate([dense, jnp.full((NA_HEADS, 1, GRID_W, GRID_W), NEG, jnp.float32)], axis=1)
    zero = jnp.zeros_like(dense)
    return jnp.concatenate([dense, zero], axis=-1), jnp.concatenate([zero, dense], axis=-1)


def _na_attention(qkv, kvc, bias_l, bias_r, bsz, seq, ctx_len):
    hp = NA_HEADS // 2
    bw = 2 * NA_DH
    tq = NA_R * GRID_W
    return pl.pallas_call(
        _na_kernel,
        out_shape=jax.ShapeDtypeStruct((bsz, seq, D), jnp.bfloat16),
        grid=(bsz, hp, seq // tq),
        in_specs=[
            pl.BlockSpec((1, tq, bw), lambda b, h, r: (b, r, h)),
            pl.BlockSpec((1, seq, bw), lambda b, h, r: (b, 0, hp + h)),
            pl.BlockSpec((1, seq, bw), lambda b, h, r: (b, 0, 2 * hp + h)),
            pl.BlockSpec((1, ctx_len, bw), lambda b, h, r: (b, 0, h)),
            pl.BlockSpec((1, ctx_len, bw), lambda b, h, r: (b, 0, hp + h)),
            pl.BlockSpec((2, 2 * NA_KH, GRID_W, 2 * GRID_W), lambda b, h, r: (h, 0, 0, 0)),
            pl.BlockSpec((2, 2 * NA_KH, GRID_W, 2 * GRID_W), lambda b, h, r: (h, 0, 0, 0)),
        ],
        out_specs=pl.BlockSpec((1, tq, bw), lambda b, h, r: (b, r, h)),
        compiler_params=_cparams(("parallel", "parallel", "arbitrary")),
        name="na_attention",
    )(qkv, qkv, qkv, kvc, kvc, bias_l, bias_r)


def _proj_res_kernel(x_ref, gate_ref, a_ref, w_ref, o_ref):
    o_ref[...] = x_ref[...] + gate_ref[0] * _dot(a_ref[...], w_ref[...])


def _proj_res(h, mods, a, w, layer, rows_per_batch, tm):
    m = h.shape[0]
    ms = functools.partial(_mod_spec, layer, tm=tm, rows_per_batch=rows_per_batch)
    return pl.pallas_call(
        _proj_res_kernel,
        out_shape=jax.ShapeDtypeStruct((m, D), jnp.float32),
        grid=(m // tm,),
        in_specs=[
            pl.BlockSpec((tm, D), lambda i: (i, 0)),
            ms(2),
            pl.BlockSpec((tm, D), lambda i: (i, 0)),
            pl.BlockSpec((D, D), lambda i: (0, 0)),
        ],
        out_specs=pl.BlockSpec((tm, D), lambda i: (i, 0)),
        compiler_params=_cparams(("parallel",)),
        name="na_out_proj",
    )(h, mods, a, w)


def _glu_kernel(x_ref, g_ref, sh_ref, sc_ref, w_ref, b_ref, o_ref):
    hn = _norm_mod(x_ref[...], g_ref[0], sh_ref[0], sc_ref[0]).astype(jnp.bfloat16)
    y = _dot(hn, w_ref[...]) + b_ref[...]
    o_ref[...] = (y[:, :D] * jax.nn.sigmoid(y[:, D:])).astype(jnp.bfloat16)


def _glu(h, mods, norm_g, w, b, layer, rows_per_batch, tm):
    m = h.shape[0]
    ms = functools.partial(_mod_spec, layer, tm=tm, rows_per_batch=rows_per_batch)
    return pl.pallas_call(
        _glu_kernel,
        out_shape=jax.ShapeDtypeStruct((m, D), jnp.bfloat16),
        grid=(m // tm,),
        in_specs=[
            pl.BlockSpec((tm, D), lambda i: (i, 0)),
            pl.BlockSpec((1, 1, D), lambda i: (layer, 0, 0)),
            ms(0), ms(1),
            pl.BlockSpec((D, 2 * D), lambda i: (0, 0)),
            pl.BlockSpec((1, 2 * D), lambda i: (0, 0)),
        ],
        out_specs=pl.BlockSpec((tm, D), lambda i: (i, 0)),
        compiler_params=_cparams(("parallel",)),
        name="conv_glu",
    )(h, norm_g, mods, mods, w, b)


def _conv_kernel(x_ref, gate_ref, a_ref, prev_ref, next_ref, wd_ref, bd_ref, lg_ref, lb_ref,
                 w2_ref, b2_ref, o_ref, win_ref, conv_ref, *, tiles_per_seq):
    tm = x_ref.shape[0]
    i = pl.program_id(0)
    pos = i % tiles_per_seq
    prev = prev_ref[...].astype(jnp.float32)
    nxt = next_ref[...].astype(jnp.float32)
    win_ref[0:HALO, :] = jnp.where(pos == 0, 0.0, prev)
    win_ref[HALO:HALO + tm, :] = a_ref[...].astype(jnp.float32)
    win_ref[HALO + tm:, :] = jnp.where(pos == tiles_per_seq - 1, 0.0, nxt)
    rc, lc = 64, 256
    for r in range(tm // rc):
        for c in range(D // lc):
            cols = slice(c * lc, (c + 1) * lc)
            acc = jnp.zeros((rc, lc), jnp.float32)
            for k in range(CONV_W):
                off = r * rc + k + HALO - CONV_PAD
                acc = acc + win_ref[off:off + rc, cols] * wd_ref[k:k + 1, cols]
            conv_ref[r * rc:(r + 1) * rc, cols] = acc
    y = _layer_norm(conv_ref[...] + bd_ref[...], lg_ref[...], lb_ref[...])
    y = (y * jax.nn.sigmoid(y)).astype(jnp.bfloat16)
    o_ref[...] = x_ref[...] + gate_ref[0] * (_dot(y, w2_ref[...]) + b2_ref[...])


def _conv_mix(h, mods, a, w_dw, b_dw, ln_g, ln_b, w_pw2, b_pw2, layer, rows_per_batch, tm):
    m = h.shape[0]
    ms = functools.partial(_mod_spec, layer, tm=tm, rows_per_batch=rows_per_batch)
    hb = tm // HALO
    last = m // HALO - 1
    vec = pl.BlockSpec((1, D), lambda i: (0, 0))
    return pl.pallas_call(
        functools.partial(_conv_kernel, tiles_per_seq=rows_per_batch // tm),
        out_shape=jax.ShapeDtypeStruct((m, D), jnp.float32),
        grid=(m // tm,),
        in_specs=[
            pl.BlockSpec((tm, D), lambda i: (i, 0)),
            ms(2),
            pl.BlockSpec((tm, D), lambda i: (i, 0)),
            pl.BlockSpec((HALO, D), lambda i: (jnp.maximum(i * hb - 1, 0), 0)),
            pl.BlockSpec((HALO, D), lambda i: (jnp.minimum((i + 1) * hb, last), 0)),
            pl.BlockSpec((CONV_W, D), lambda i: (0, 0)),
            vec, vec, vec,
            pl.BlockSpec((D, D), lambda i: (0, 0)),
            vec,
        ],
        out_specs=pl.BlockSpec((tm, D), lambda i: (i, 0)),
        scratch_shapes=[pltpu.VMEM((tm + 2 * HALO, D), jnp.float32), pltpu.VMEM((tm, D), jnp.float32)],
        compiler_params=_cparams(("parallel",)),
        name="conv_mix",
    )(h, mods, a, a, a, w_dw, b_dw, ln_g, ln_b, w_pw2, b_pw2)


def kernel(x, c, ctx, c_ctx, ada_w, ada_b, norm1_g, norm2_g, a_w_in, a_b_in, a_ln_g, a_ln_b, a_w_s, a_b_s, a_w_out, b_w_qkv, b_w_o, b_q_g, b_k_g, b_rpb, c_w_pw1, c_b_pw1, c_w_dw, c_b_dw, c_ln_g, c_ln_b, c_w_pw2, c_b_pw2, mlp_w1, mlp_w2):
    bsz, seq, _ = x.shape
    ctx_len = ctx.shape[1]
    bf = jnp.bfloat16
    assert bsz <= CTX_ROW and seq % (NA_R * GRID_W) == 0 and seq // GRID_W >= NA_BAND

    cond = jnp.zeros((MOD_ROWS, D), jnp.float32).at[:bsz].set(c).at[CTX_ROW].set(c_ctx)
    mods = _ada_mods(cond, ada_w, ada_b)
    n1 = norm1_g.reshape(DEPTH, 1, D)
    n2 = norm2_g.reshape(DEPTH, 1, D)

    h = x.reshape(bsz * seq, D)
    hc = ctx.reshape(bsz * ctx_len, D)
    tm, tmc = 1024, 256

    for l in range(DEPTH):
        kind, j = l % N_MIXERS, l // N_MIXERS
        w1, w2 = mlp_w1[l].astype(bf), mlp_w2[l].astype(bf)
        if kind == 0:
            w_in, w_out, w_s = a_w_in[j].astype(bf), a_w_out[j].astype(bf), a_w_s[j].astype(bf)
            b_in = a_b_in[j].reshape(1, -1)
            ln_g, ln_b = a_ln_g[j].reshape(1, -1), a_ln_b[j].reshape(1, -1)
            bs_full = jnp.repeat(a_b_s[j].T, SG_GW, axis=1)
            uv = _sgu_in(h, mods, n1, w_in, b_in, ln_g, ln_b, l, seq, tm)
            h = _sgu_out(h, mods, uv, w_s, bs_full, w_out, l, seq, 512)
            if l < LAST_CTX_READER:
                uvc = _sgu_in(hc, mods, n1, w_in, b_in, ln_g, ln_b, l, None, tmc)
                hc = _sgu_out(hc, mods, uvc, w_s, bs_full, w_out, l, None, tmc)
                hc = _mlp(hc, mods, n2, w1, w2, l, None, tmc)
        elif kind == 1:
            w_qkv, w_o = b_w_qkv[j].astype(bf), b_w_o[j].astype(bf)
            gains = jnp.stack([jnp.tile(b_q_g[j], NA_HEADS) * NA_DH ** -0.5,
                               jnp.tile(b_k_g[j], NA_HEADS),
                               jnp.ones((D,), jnp.float32)]).reshape(3, 1, D)
            bias_l, bias_r = _na_bias_tables(b_rpb[j])
            qkv = _qkv(h, mods, n1, w_qkv, gains, l, seq, tm, 0)
            kvc = _qkv(hc, mods, n1, w_qkv, gains, l, None, tmc, 1)
            o = _na_attention(qkv.reshape(bsz, seq, 3 * D), kvc.reshape(bsz, ctx_len, 2 * D),
                              bias_l, bias_r, bsz, seq, ctx_len)
            h = _proj_res(h, mods, o.reshape(bsz * seq, D), w_o, l, seq, tm)
        else:
            a = _glu(h, mods, n1, c_w_pw1[j].astype(bf), c_b_pw1[j].reshape(1, -1), l, seq, tm)
            h = _conv_mix(h, mods, a, c_w_dw[j], c_b_dw[j].reshape(1, -1), c_ln_g[j].reshape(1, -1),
                          c_ln_b[j].reshape(1, -1), c_w_pw2[j].astype(bf), c_b_pw2[j].reshape(1, -1),
                          l, seq, 256)
        h = _mlp(h, mods, n2, w1, w2, l, seq, tm)
    return h.reshape(bsz, seq, D)
```

```python
import functools

import jax
import jax.numpy as jnp
import numpy as np
from jax.experimental import pallas as pl
from jax.experimental.pallas import tpu as pltpu

D = 1024
DEPTH = 4
GRID_W = 64
N_MIXERS = 3
CHUNK = 128
SG_GROUPS = 8
SG_HALF = 2 * D
SG_GW = SG_HALF // SG_GROUPS
NA_HEADS = 16
NA_DH = D // NA_HEADS
NA_KH = 8
NA_KW = 16
CONV_W = 31
CONV_PAD = CONV_W // 2
D_FF = 4 * D
EPS = 1e-6
LAST_CTX_READER = max(range(1, DEPTH, N_MIXERS))

MOD_ROWS = 16
CTX_ROW = 8
NEG = -1e30
VMEM_LIMIT = 56 * 1024 * 1024

NA_R = 4
NA_BAND = 12
HALO = 16
SUBLANES, LANES = 8, 128
PHASES = 4
CONV_UNROLL = 2


def _cparams(sem):
    return pltpu.CompilerParams(dimension_semantics=sem, vmem_limit_bytes=VMEM_LIMIT)


def _norm_mod(x, g, sh, sc):
    ms = jnp.mean(x * x, axis=-1, keepdims=True)
    return (x * jax.lax.rsqrt(ms + EPS) * g) * (1.0 + sc) + sh


def _layer_norm(x, g, b):
    mu = jnp.mean(x, axis=-1, keepdims=True)
    xc = x - mu
    var = jnp.mean(xc * xc, axis=-1, keepdims=True)
    return xc * jax.lax.rsqrt(var + EPS) * g + b


def _dot(a, b):
    return jnp.dot(a, b, preferred_element_type=jnp.float32)


def _ada_kernel(c_ref, w_ref, b_ref, o_ref):
    cond = c_ref[...]
    act = (cond * jax.nn.sigmoid(cond)).astype(jnp.bfloat16)
    o_ref[0] = _dot(act, w_ref[0].astype(jnp.bfloat16)) + b_ref[0]


def _ada_mods(cond, ada_w, ada_b):
    tn = 1536
    out = pl.pallas_call(
        _ada_kernel,
        out_shape=jax.ShapeDtypeStruct((DEPTH, MOD_ROWS, 6 * D), jnp.float32),
        grid=(DEPTH, 6 * D // tn),
        in_specs=[
            pl.BlockSpec((MOD_ROWS, D), lambda l, n: (0, 0)),
            pl.BlockSpec((1, D, tn), lambda l, n: (l, 0, n)),
            pl.BlockSpec((1, 1, tn), lambda l, n: (l, 0, n)),
        ],
        out_specs=pl.BlockSpec((1, MOD_ROWS, tn), lambda l, n: (l, 0, n)),
        compiler_params=_cparams(("arbitrary", "arbitrary")),
        name="ada_mods",
    )(cond, ada_w, ada_b.reshape(DEPTH, 1, 6 * D))
    return out.reshape(DEPTH * MOD_ROWS, 1, 6 * D)


def _mod_spec(layer, chunk, tm, rows_per_batch):
    if rows_per_batch is None:
        return pl.BlockSpec((1, 1, D), lambda i, *_: (layer * MOD_ROWS + CTX_ROW, 0, chunk))
    return pl.BlockSpec(
        (1, 1, D), lambda i, *_: (layer * MOD_ROWS + (i * tm) // rows_per_batch, 0, chunk))


def _mlp_kernel(x_ref, g_ref, sh_ref, sc_ref, gate_ref, w1_ref, w2_ref, o_ref, *, tf):
    hn = _norm_mod(x_ref[...], g_ref[0], sh_ref[0], sc_ref[0]).astype(jnp.bfloat16)
    acc = None
    for f in range(D_FF // tf):
        a = jnp.maximum(_dot(hn, w1_ref[:, f * tf:(f + 1) * tf]), 0.0)
        part = _dot((a * a).astype(jnp.bfloat16), w2_ref[f * tf:(f + 1) * tf, :])
        acc = part if acc is None else acc + part
    o_ref[...] = x_ref[...] + gate_ref[0] * acc


def _resident(shape):
    return pl.BlockSpec(shape, lambda i: (0,) * len(shape), pipeline_mode=pl.Buffered(1))


def _mlp(h, mods, norm_g, w1, w2, layer, rows_per_batch, tm, tf=1024):
    m = h.shape[0]
    ms = functools.partial(_mod_spec, layer, tm=tm, rows_per_batch=rows_per_batch)
    return pl.pallas_call(
        functools.partial(_mlp_kernel, tf=tf),
        out_shape=jax.ShapeDtypeStruct((m, D), jnp.float32),
        grid=(m // tm,),
        in_specs=[
            pl.BlockSpec((tm, D), lambda i: (i, 0)),
            pl.BlockSpec((1, 1, D), lambda i: (layer, 0, 0)),
            ms(3), ms(4), ms(5),
            _resident((D, D_FF)),
            _resident((D_FF, D)),
        ],
        out_specs=pl.BlockSpec((tm, D), lambda i: (i, 0)),
        compiler_params=_cparams(("parallel",)),
        name="sq_relu_mlp",
    )(h, norm_g, mods, mods, mods, w1, w2)


def _sgu_in_kernel(x_ref, g_ref, sh_ref, sc_ref, w_ref, b_ref, lg_ref, lb_ref, o_ref):
    hn = _norm_mod(x_ref[...], g_ref[0], sh_ref[0], sc_ref[0]).astype(jnp.bfloat16)
    v = jax.nn.gelu(_dot(hn, w_ref[:, SG_HALF:]) + b_ref[:, SG_HALF:])
    o_ref[:, SG_HALF:] = _layer_norm(v, lg_ref[...], lb_ref[...]).astype(jnp.bfloat16)
    u = jax.nn.gelu(_dot(hn, w_ref[:, :SG_HALF]) + b_ref[:, :SG_HALF])
    o_ref[:, :SG_HALF] = u.astype(jnp.bfloat16)


def _sgu_in(h, mods, norm_g, w_in, b_in, ln_g, ln_b, layer, rows_per_batch, tm):
    m = h.shape[0]
    ms = functools.partial(_mod_spec, layer, tm=tm, rows_per_batch=rows_per_batch)
    return pl.pallas_call(
        _sgu_in_kernel,
        out_shape=jax.ShapeDtypeStruct((m, 2 * SG_HALF), jnp.bfloat16),
        grid=(m // tm,),
        in_specs=[
            pl.BlockSpec((tm, D), lambda i: (i, 0)),
            pl.BlockSpec((1, 1, D), lambda i: (layer, 0, 0)),
            ms(0), ms(1),
            _resident((D, 2 * SG_HALF)),
            _resident((1, 2 * SG_HALF)),
            _resident((1, SG_HALF)),
            _resident((1, SG_HALF)),
        ],
        out_specs=pl.BlockSpec((tm, 2 * SG_HALF), lambda i: (i, 0)),
        compiler_params=_cparams(("parallel",)),
        name="sgu_in",
    )(h, norm_g, mods, mods, w_in, b_in, ln_g, ln_b)


def _sgu_out_kernel(x_ref, gate_ref, u_ref, v_ref, ws_ref, bs_ref, wo_ref, o_ref, gated_ref):
    tm = x_ref.shape[0]
    for ci in range(tm // CHUNK):
        rows = slice(ci * CHUNK, (ci + 1) * CHUNK)
        for g in range(SG_GROUPS):
            cols = slice(g * SG_GW, (g + 1) * SG_GW)
            vs = _dot(ws_ref[g], v_ref[rows, cols]) + bs_ref[:, cols]
            gated_ref[rows, cols] = (u_ref[rows, cols].astype(jnp.float32) * vs).astype(jnp.bfloat16)
    o_ref[...] = x_ref[...] + gate_ref[0] * _dot(gated_ref[...], wo_ref[...])


def _sgu_out(h, mods, uv, w_s, bs_full, w_out, layer, rows_per_batch, tm):
    m = h.shape[0]
    ms = functools.partial(_mod_spec, layer, tm=tm, rows_per_batch=rows_per_batch)
    return pl.pallas_call(
        _sgu_out_kernel,
        out_shape=jax.ShapeDtypeStruct((m, D), jnp.float32),
        grid=(m // tm,),
        in_specs=[
            pl.BlockSpec((tm, D), lambda i: (i, 0)),
            ms(2),
            pl.BlockSpec((tm, SG_HALF), lambda i: (i, 0)),
            pl.BlockSpec((tm, SG_HALF), lambda i: (i, 1)),
            pl.BlockSpec((SG_GROUPS, CHUNK, CHUNK), lambda i: (0, 0, 0)),
            pl.BlockSpec((CHUNK, SG_HALF), lambda i: (0, 0)),
            pl.BlockSpec((SG_HALF, D), lambda i: (0, 0)),
        ],
        out_specs=pl.BlockSpec((tm, D), lambda i: (i, 0)),
        scratch_shapes=[pltpu.VMEM((tm, SG_HALF), jnp.bfloat16)],
        compiler_params=_cparams(("parallel",)),
        name="sgu_out",
    )(h, mods, uv, uv, w_s, bs_full, w_out)


def _qkv_kernel(x_ref, g_ref, sh_ref, sc_ref, w_ref, hg_ref, o_ref, *, n_norm):
    hn = _norm_mod(x_ref[...], g_ref[0], sh_ref[0], sc_ref[0]).astype(jnp.bfloat16)
    lane = jax.lax.broadcasted_iota(jnp.int32, (1, 2 * NA_DH), 1)
    first = lane < NA_DH
    for n in range(w_ref.shape[1] // D):
        y = _dot(hn, w_ref[:, n * D:(n + 1) * D])
        if n >= n_norm:
            o_ref[:, n * D:(n + 1) * D] = y.astype(jnp.bfloat16)
            continue
        for j in range(D // (2 * NA_DH)):
            t = y[:, j * 2 * NA_DH:(j + 1) * 2 * NA_DH]
            cols = slice(n * D + j * 2 * NA_DH, n * D + (j + 1) * 2 * NA_DH)
            t2 = t * t
            s_all = jnp.sum(t2, axis=-1, keepdims=True)
            s_first = jnp.sum(jnp.where(first, t2, 0.0), axis=-1, keepdims=True)
            ms = jnp.where(first, s_first, s_all - s_first) * (1.0 / NA_DH)
            o_ref[:, cols] = (t * jax.lax.rsqrt(ms + EPS) * hg_ref[:, cols]).astype(jnp.bfloat16)


def _qkv(h, mods, norm_g, w, head_gains, layer, rows_per_batch, tm):
    m = h.shape[0]
    nd = w.shape[1]
    ms = functools.partial(_mod_spec, layer, tm=tm, rows_per_batch=rows_per_batch)
    return pl.pallas_call(
        functools.partial(_qkv_kernel, n_norm=nd // D - 1),
        out_shape=jax.ShapeDtypeStruct((m, nd), jnp.bfloat16),
        grid=(m // tm,),
        in_specs=[
            pl.BlockSpec((tm, D), lambda i: (i, 0)),
            pl.BlockSpec((1, 1, D), lambda i: (layer, 0, 0)),
            ms(0), ms(1),
            _resident((D, nd)),
            _resident((1, nd)),
        ],
        out_specs=pl.BlockSpec((tm, nd), lambda i: (i, 0)),
        compiler_params=_cparams(("parallel",)),
        name="na_qkv",
    )(h, norm_g, mods, mods, w, head_gains)


def _na_kernel(q_ref, k_ref, v_ref, kc_ref, vc_ref, bl_ref, br_ref, o_ref):
    rb = pl.program_id(2)
    rows = k_ref.shape[1] // GRID_W
    r0 = rb * NA_R
    kb = jnp.clip(r0 - NA_KH // 2, 0, rows - NA_BAND)
    start = pl.multiple_of(kb * GRID_W, GRID_W)
    kband = k_ref[0, pl.ds(start, NA_BAND * GRID_W), :]
    vband = v_ref[0, pl.ds(start, NA_BAND * GRID_W), :]
    kc = kc_ref[0]
    vc = vc_ref[0]
    q = q_ref[0]
    lane = jax.lax.broadcasted_iota(jnp.int32, (1, 2 * NA_DH), 1)
    first = lane < NA_DH
    nt = (((1,), (1,)), ((), ()))
    outs = []
    for hh in range(2):
        qm = jnp.where(first if hh == 0 else jnp.logical_not(first), q, jnp.zeros_like(q))
        s_loc = jax.lax.dot_general(qm, kband, nt, preferred_element_type=jnp.float32)
        s_ctx = jax.lax.dot_general(qm, kc, nt, preferred_element_type=jnp.float32)
        bias_rows = []
        for i in range(NA_R):
            r = r0 + i
            rs = jnp.clip(r - NA_KH // 2, 0, rows - NA_KH)
            pieces = []
            for jp in range(NA_BAND // 2):
                idx = []
                for j in (2 * jp, 2 * jp + 1):
                    kr = kb + j
                    valid = jnp.logical_and(kr >= rs, kr < rs + NA_KH)
                    idx.append(jnp.where(valid, kr - r + (NA_KH - 1), 2 * NA_KH - 1))
                pieces.append(bl_ref[hh, idx[0]] + br_ref[hh, idx[1]])
            bias_rows.append(jnp.concatenate(pieces, axis=1))
        s_loc = s_loc + jnp.concatenate(bias_rows, axis=0)
        mx = jnp.maximum(jnp.max(s_loc, axis=-1, keepdims=True), jnp.max(s_ctx, axis=-1, keepdims=True))
        p_loc = jnp.exp(s_loc - mx)
        p_ctx = jnp.exp(s_ctx - mx)
        den = jnp.sum(p_loc, axis=-1, keepdims=True) + jnp.sum(p_ctx, axis=-1, keepdims=True)
        o = _dot(p_loc.astype(jnp.bfloat16), vband) + _dot(p_ctx.astype(jnp.bfloat16), vc)
        outs.append(o / den)
    o_ref[0] = jnp.where(first, outs[0], outs[1]).astype(jnp.bfloat16)


def _na_bias_tables(rpb):
    cols = np.arange(GRID_W)
    cs = np.clip(cols - NA_KW // 2, 0, GRID_W - NA_KW)
    rel = cols[None, :] - cols[:, None] + (NA_KW - 1)
    ok = (cols[None, :] >= cs[:, None]) & (cols[None, :] < cs[:, None] + NA_KW)
    dense = jnp.where(ok[None, None], rpb[:, :, np.clip(rel, 0, 2 * NA_KW - 2)], NEG)
    dense = jnp.concatenate([dense, jnp.full((NA_HEADS, 1, GRID_W, GRID_W), NEG, jnp.float32)], axis=1)
    zero = jnp.zeros_like(dense)
    return jnp.concatenate([dense, zero], axis=-1), jnp.concatenate([zero, dense], axis=-1)


def _na_attention(qkv, kvc, bias_l, bias_r, bsz, seq, ctx_len):
    hp = NA_HEADS // 2
    bw = 2 * NA_DH
    tq = NA_R * GRID_W
    return pl.pallas_call(
        _na_kernel,
        out_shape=jax.ShapeDtypeStruct((bsz, seq, D), jnp.bfloat16),
        grid=(bsz, hp, seq // tq),
        in_specs=[
            pl.BlockSpec((1, tq, bw), lambda b, h, r: (b, r, h)),
            pl.BlockSpec((1, seq, bw), lambda b, h, r: (b, 0, hp + h)),
            pl.BlockSpec((1, seq, bw), lambda b, h, r: (b, 0, 2 * hp + h)),
            pl.BlockSpec((1, ctx_len, bw), lambda b, h, r: (b, 0, h)),
            pl.BlockSpec((1, ctx_len, bw), lambda b, h, r: (b, 0, hp + h)),
            pl.BlockSpec((2, 2 * NA_KH, GRID_W, 2 * GRID_W), lambda b, h, r: (h, 0, 0, 0)),
            pl.BlockSpec((2, 2 * NA_KH, GRID_W, 2 * GRID_W), lambda b, h, r: (h, 0, 0, 0)),
        ],
        out_specs=pl.BlockSpec((1, tq, bw), lambda b, h, r: (b, r, h)),
        compiler_params=_cparams(("parallel", "parallel", "arbitrary")),
        name="na_attention",
    )(qkv, qkv, qkv, kvc, kvc, bias_l, bias_r)


def _proj_res_kernel(x_ref, gate_ref, a_ref, w_ref, o_ref):
    o_ref[...] = x_ref[...] + gate_ref[0] * _dot(a_ref[...], w_ref[...])


def _proj_res(h, mods, a, w, layer, rows_per_batch, tm):
    m = h.shape[0]
    ms = functools.partial(_mod_spec, layer, tm=tm, rows_per_batch=rows_per_batch)
    return pl.pallas_call(
        _proj_res_kernel,
        out_shape=jax.ShapeDtypeStruct((m, D), jnp.float32),
        grid=(m // tm,),
        in_specs=[
            pl.BlockSpec((tm, D), lambda i: (i, 0)),
            ms(2),
            pl.BlockSpec((tm, D), lambda i: (i, 0)),
            pl.BlockSpec((D, D), lambda i: (0, 0)),
        ],
        out_specs=pl.BlockSpec((tm, D), lambda i: (i, 0)),
        compiler_params=_cparams(("parallel",)),
        name="na_out_proj",
    )(h, mods, a, w)


def _glu_kernel(x_ref, g_ref, sh_ref, sc_ref, w_ref, b_ref, o_ref):
    hn = _norm_mod(x_ref[...], g_ref[0], sh_ref[0], sc_ref[0]).astype(jnp.bfloat16)
    y = _dot(hn, w_ref[...]) + b_ref[...]
    o_ref[...] = (y[:, :D] * jax.nn.sigmoid(y[:, D:])).astype(jnp.bfloat16)


def _glu(h, mods, norm_g, w, b, layer, rows_per_batch, tm):
    m = h.shape[0]
    ms = functools.partial(_mod_spec, layer, tm=tm, rows_per_batch=rows_per_batch)
    return pl.pallas_call(
        _glu_kernel,
        out_shape=jax.ShapeDtypeStruct((m, D), jnp.bfloat16),
        grid=(m // tm,),
        in_specs=[
            pl.BlockSpec((tm, D), lambda i: (i, 0)),
            pl.BlockSpec((1, 1, D), lambda i: (layer, 0, 0)),
            ms(0), ms(1),
            pl.BlockSpec((D, 2 * D), lambda i: (0, 0)),
            pl.BlockSpec((1, 2 * D), lambda i: (0, 0)),
        ],
        out_specs=pl.BlockSpec((tm, D), lambda i: (i, 0)),
        compiler_params=_cparams(("parallel",)),
        name="conv_glu",
    )(h, norm_g, mods, mods, w, b)


def _conv_kernel(x_ref, gate_ref, a_ref, prev_ref, next_ref, wd_ref, bd_ref, lg_ref, lb_ref,
                 w2_ref, b2_ref, o_ref, win_ref, conv_ref, *, tiles_per_seq):
    tm = x_ref.shape[0]
    i = pl.program_id(0)
    pos = i % tiles_per_seq
    prev = jnp.where(pos == 0, 0.0, prev_ref[...].astype(jnp.float32))
    nxt = jnp.where(pos == tiles_per_seq - 1, 0.0, next_ref[...].astype(jnp.float32))
    a = a_ref[...].astype(jnp.float32)
    n_slab = D // LANES
    for s in range(n_slab):
        cols = slice(s * LANES, (s + 1) * LANES)
        win_ref[s, 0:HALO, :] = prev[:, cols]
        win_ref[s, HALO:HALO + tm, :] = a[:, cols]
        win_ref[s, HALO + tm:, :] = nxt[:, cols]
    blk = PHASES * SUBLANES
    for s in range(n_slab):
        cols = slice(s * LANES, (s + 1) * LANES)
        taps = [jnp.broadcast_to(wd_ref[k:k + 1, cols], (SUBLANES, LANES)) for k in range(CONV_W)]

        def block(rb, carry, s=s, taps=taps):
            base = pl.multiple_of(rb * (CONV_UNROLL * blk), CONV_UNROLL * blk)
            for q in range(CONV_UNROLL * PHASES):
                row = (q // PHASES) * blk + q % PHASES
                acc = None
                for k in range(CONV_W):
                    off = row + k + HALO - CONV_PAD
                    term = win_ref[s, pl.ds(base + off, SUBLANES, stride=PHASES), :] * taps[k]
                    acc = term if acc is None else acc + term
                conv_ref[s, pl.ds(base + row, SUBLANES, stride=PHASES), :] = acc
            return carry

        jax.lax.fori_loop(0, tm // (CONV_UNROLL * blk), block, 0)
    conv = jnp.concatenate([conv_ref[s] for s in range(n_slab)], axis=1)
    y = _layer_norm(conv + bd_ref[...], lg_ref[...], lb_ref[...])
    y = (y * jax.nn.sigmoid(y)).astype(jnp.bfloat16)
    o_ref[...] = x_ref[...] + gate_ref[0] * (_dot(y, w2_ref[...]) + b2_ref[...])


def _conv_mix(h, mods, a, w_dw, b_dw, ln_g, ln_b, w_pw2, b_pw2, layer, rows_per_batch, tm):
    m = h.shape[0]
    ms = functools.partial(_mod_spec, layer, tm=tm, rows_per_batch=rows_per_batch)
    hb = tm // HALO
    last = m // HALO - 1
    vec = pl.BlockSpec((1, D), lambda i: (0, 0))
    return pl.pallas_call(
        functools.partial(_conv_kernel, tiles_per_seq=rows_per_batch // tm),
        out_shape=jax.ShapeDtypeStruct((m, D), jnp.float32),
        grid=(m // tm,),
        in_specs=[
            pl.BlockSpec((tm, D), lambda i: (i, 0)),
            ms(2),
            pl.BlockSpec((tm, D), lambda i: (i, 0)),
            pl.BlockSpec((HALO, D), lambda i: (jnp.maximum(i * hb - 1, 0), 0)),
            pl.BlockSpec((HALO, D), lambda i: (jnp.minimum((i + 1) * hb, last), 0)),
            pl.BlockSpec((CONV_W, D), lambda i: (0, 0)),
            vec, vec, vec,
            pl.BlockSpec((D, D), lambda i: (0, 0)),
            vec,
        ],
        out_specs=pl.BlockSpec((tm, D), lambda i: (i, 0)),
        scratch_shapes=[pltpu.VMEM((D // LANES, tm + 2 * HALO, LANES), jnp.float32),
                        pltpu.VMEM((D // LANES, tm, LANES), jnp.float32)],
        compiler_params=_cparams(("parallel",)),
        name="conv_mix",
    )(h, mods, a, a, a, w_dw, b_dw, ln_g, ln_b, w_pw2, b_pw2)


def kernel(x, c, ctx, c_ctx, ada_w, ada_b, norm1_g, norm2_g, a_w_in, a_b_in, a_ln_g, a_ln_b, a_w_s, a_b_s, a_w_out, b_w_qkv, b_w_o, b_q_g, b_k_g, b_rpb, c_w_pw1, c_b_pw1, c_w_dw, c_b_dw, c_ln_g, c_ln_b, c_w_pw2, c_b_pw2, mlp_w1, mlp_w2):
    bsz, seq, _ = x.shape
    ctx_len = ctx.shape[1]
    bf = jnp.bfloat16
    assert bsz <= CTX_ROW and seq % (NA_R * GRID_W) == 0 and seq // GRID_W >= NA_BAND

    cond = jnp.zeros((MOD_ROWS, D), jnp.float32).at[:bsz].set(c).at[CTX_ROW].set(c_ctx)
    mods = _ada_mods(cond, ada_w, ada_b)
    n1 = norm1_g.reshape(DEPTH, 1, D)
    n2 = norm2_g.reshape(DEPTH, 1, D)

    h = x.reshape(bsz * seq, D)
    hc = ctx.reshape(bsz * ctx_len, D)
    tm, tmc = 512, 256

    for l in range(DEPTH):
        kind, j = l % N_MIXERS, l // N_MIXERS
        w1, w2 = mlp_w1[l].astype(bf), mlp_w2[l].astype(bf)
        if kind == 0:
            w_in, w_out, w_s = a_w_in[j].astype(bf), a_w_out[j].astype(bf), a_w_s[j].astype(bf)
            b_in = a_b_in[j].reshape(1, -1)
            ln_g, ln_b = a_ln_g[j].reshape(1, -1), a_ln_b[j].reshape(1, -1)
            bs_full = jnp.repeat(a_b_s[j].T, SG_GW, axis=1)
            uv = _sgu_in(h, mods, n1, w_in, b_in, ln_g, ln_b, l, seq, tm)
            h = _sgu_out(h, mods, uv, w_s, bs_full, w_out, l, seq, 512)
            if l < LAST_CTX_READER:
                uvc = _sgu_in(hc, mods, n1, w_in, b_in, ln_g, ln_b, l, None, tmc)
                hc = _sgu_out(hc, mods, uvc, w_s, bs_full, w_out, l, None, tmc)
                hc = _mlp(hc, mods, n2, w1, w2, l, None, tmc)
        elif kind == 1:
            w_qkv, w_o = b_w_qkv[j].astype(bf), b_w_o[j].astype(bf)
            gains = jnp.concatenate([jnp.tile(b_q_g[j], NA_HEADS) * NA_DH ** -0.5,
                                     jnp.tile(b_k_g[j], NA_HEADS),
                                     jnp.ones((D,), jnp.float32)]).reshape(1, 3 * D)
            bias_l, bias_r = _na_bias_tables(b_rpb[j])
            qkv = _qkv(h, mods, n1, w_qkv, gains, l, seq, tm)
            kvc = _qkv(hc, mods, n1, w_qkv[:, D:], gains[:, D:], l, None, tmc)
            o = _na_attention(qkv.reshape(bsz, seq, 3 * D), kvc.reshape(bsz, ctx_len, 2 * D),
                              bias_l, bias_r, bsz, seq, ctx_len)
            h = _proj_res(h, mods, o.reshape(bsz * seq, D), w_o, l, seq, tm)
        else:
            a = _glu(h, mods, n1, c_w_pw1[j].astype(bf), c_b_pw1[j].reshape(1, -1), l, seq, tm)
            h = _conv_mix(h, mods, a, c_w_dw[j], c_b_dw[j].reshape(1, -1), c_ln_g[j].reshape(1, -1),
                          c_ln_b[j].reshape(1, -1), c_w_pw2[j].astype(bf), c_b_pw2[j].reshape(1, -1),
                          l, seq, 256)
        h = _mlp(h, mods, n2, w1, w2, l, seq, tm)
    return h.reshape(bsz, seq, D)
```

```python
import functools

import jax
import jax.numpy as jnp
import numpy as np
from jax.experimental import pallas as pl
from jax.experimental.pallas import tpu as pltpu

D = 1024
DEPTH = 4
GRID_W = 64
N_MIXERS = 3
CHUNK = 128
SG_GROUPS = 8
SG_HALF = 2 * D
SG_GW = SG_HALF // SG_GROUPS
SG_IN_CHUNK = 512
NA_HEADS = 16
NA_DH = D // NA_HEADS
NA_KH = 8
NA_KW = 16
CONV_W = 31
CONV_PAD = CONV_W // 2
D_FF = 4 * D
EPS = 1e-6
LAST_CTX_READER = max(range(1, DEPTH, N_MIXERS))

MOD_ROWS = 16
CTX_ROW = 8
NEG = -1e30
VMEM_LIMIT = 56 * 1024 * 1024

NA_R = 4
NA_BAND = 12
HALO = 16
SUBLANES, LANES = 8, 128
PHASES = 4
CONV_UNROLL = 2


def _cparams(sem):
    return pltpu.CompilerParams(dimension_semantics=sem, vmem_limit_bytes=VMEM_LIMIT)


def _norm_mod(x, g, sh, sc):
    ms = jnp.mean(x * x, axis=-1, keepdims=True)
    return x * jax.lax.rsqrt(ms + EPS) * (g * (1.0 + sc)) + sh


def _gelu_tanh(x):
    c = -2.0 * np.sqrt(2.0 / np.pi) * np.log2(np.e)
    return x / (1.0 + jnp.exp2(x * (c + (0.044715 * c) * (x * x))))


def _layer_norm(x, g, b):
    mu = jnp.mean(x, axis=-1, keepdims=True)
    xc = x - mu
    var = jnp.mean(xc * xc, axis=-1, keepdims=True)
    return xc * jax.lax.rsqrt(var + EPS) * g + b


def _dot(a, b):
    return jnp.dot(a, b, preferred_element_type=jnp.float32)


def _ada_kernel(c_ref, w_ref, b_ref, o_ref):
    cond = c_ref[...]
    act = (cond * jax.nn.sigmoid(cond)).astype(jnp.bfloat16)
    o_ref[0] = _dot(act, w_ref[0].astype(jnp.bfloat16)) + b_ref[0]


def _ada_mods(cond, ada_w, ada_b):
    tn = 1536
    out = pl.pallas_call(
        _ada_kernel,
        out_shape=jax.ShapeDtypeStruct((DEPTH, MOD_ROWS, 6 * D), jnp.float32),
        grid=(DEPTH, 6 * D // tn),
        in_specs=[
            pl.BlockSpec((MOD_ROWS, D), lambda l, n: (0, 0)),
            pl.BlockSpec((1, D, tn), lambda l, n: (l, 0, n)),
            pl.BlockSpec((1, 1, tn), lambda l, n: (l, 0, n)),
        ],
        out_specs=pl.BlockSpec((1, MOD_ROWS, tn), lambda l, n: (l, 0, n)),
        compiler_params=_cparams(("arbitrary", "arbitrary")),
        name="ada_mods",
    )(cond, ada_w, ada_b.reshape(DEPTH, 1, 6 * D))
    return out.reshape(DEPTH * MOD_ROWS, 1, 6 * D)


def _mod_spec(layer, chunk, tm, rows_per_batch):
    if rows_per_batch is None:
        return pl.BlockSpec((1, 1, D), lambda i, *_: (layer * MOD_ROWS + CTX_ROW, 0, chunk))
    return pl.BlockSpec(
        (1, 1, D), lambda i, *_: (layer * MOD_ROWS + (i * tm) // rows_per_batch, 0, chunk))


def _mlp_kernel(x_ref, g_ref, sh_ref, sc_ref, gate_ref, w1_ref, w2_ref, o_ref, *, tf):
    hn = _norm_mod(x_ref[...], g_ref[0], sh_ref[0], sc_ref[0]).astype(jnp.bfloat16)
    acc = None
    for f in range(D_FF // tf):
        a = jnp.maximum(_dot(hn, w1_ref[:, f * tf:(f + 1) * tf]), 0.0)
        part = _dot((a * a).astype(jnp.bfloat16), w2_ref[f * tf:(f + 1) * tf, :])
        acc = part if acc is None else acc + part
    o_ref[...] = x_ref[...] + gate_ref[0] * acc


def _resident(shape):
    return pl.BlockSpec(shape, lambda i: (0,) * len(shape), pipeline_mode=pl.Buffered(1))


def _mlp(h, mods, norm_g, w1, w2, layer, rows_per_batch, tm, tf=1024):
    m = h.shape[0]
    ms = functools.partial(_mod_spec, layer, tm=tm, rows_per_batch=rows_per_batch)
    return pl.pallas_call(
        functools.partial(_mlp_kernel, tf=tf),
        out_shape=jax.ShapeDtypeStruct((m, D), jnp.float32),
        grid=(m // tm,),
        in_specs=[
            pl.BlockSpec((tm, D), lambda i: (i, 0)),
            pl.BlockSpec((1, 1, D), lambda i: (layer, 0, 0)),
            ms(3), ms(4), ms(5),
            _resident((D, D_FF)),
            _resident((D_FF, D)),
        ],
        out_specs=pl.BlockSpec((tm, D), lambda i: (i, 0)),
        compiler_params=_cparams(("parallel",)),
        name="sq_relu_mlp",
    )(h, norm_g, mods, mods, mods, w1, w2)


def _sgu_in_kernel(x_ref, g_ref, sh_ref, sc_ref, w_ref, b_ref, lg_ref, lb_ref, o_ref):
    hn = _norm_mod(x_ref[...], g_ref[0], sh_ref[0], sc_ref[0]).astype(jnp.bfloat16)
    tn = SG_IN_CHUNK

    def act(c):
        return _gelu_tanh(_dot(hn, w_ref[:, c * tn:(c + 1) * tn]) + b_ref[:, c * tn:(c + 1) * tn])

    vs = [act(c) for c in range(SG_HALF // tn, 2 * SG_HALF // tn)]
    mu = sum(jnp.sum(v, axis=-1, keepdims=True) for v in vs) * (1.0 / SG_HALF)
    vs = [v - mu for v in vs]
    var = sum(jnp.sum(v * v, axis=-1, keepdims=True) for v in vs) * (1.0 / SG_HALF)
    rstd = jax.lax.rsqrt(var + EPS)
    for c, v in enumerate(vs):
        cols = slice(c * tn, (c + 1) * tn)
        o_ref[:, SG_HALF + c * tn:SG_HALF + (c + 1) * tn] = (
            v * rstd * lg_ref[:, cols] + lb_ref[:, cols]).astype(jnp.bfloat16)
    for c in range(SG_HALF // tn):
        o_ref[:, c * tn:(c + 1) * tn] = act(c).astype(jnp.bfloat16)


def _sgu_in(h, mods, norm_g, w_in, b_in, ln_g, ln_b, layer, rows_per_batch, tm):
    m = h.shape[0]
    ms = functools.partial(_mod_spec, layer, tm=tm, rows_per_batch=rows_per_batch)
    return pl.pallas_call(
        _sgu_in_kernel,
        out_shape=jax.ShapeDtypeStruct((m, 2 * SG_HALF), jnp.bfloat16),
        grid=(m // tm,),
        in_specs=[
            pl.BlockSpec((tm, D), lambda i: (i, 0)),
            pl.BlockSpec((1, 1, D), lambda i: (layer, 0, 0)),
            ms(0), ms(1),
            _resident((D, 2 * SG_HALF)),
            _resident((1, 2 * SG_HALF)),
            _resident((1, SG_HALF)),
            _resident((1, SG_HALF)),
        ],
        out_specs=pl.BlockSpec((tm, 2 * SG_HALF), lambda i: (i, 0)),
        compiler_params=_cparams(("parallel",)),
        name="sgu_in",
    )(h, norm_g, mods, mods, w_in, b_in, ln_g, ln_b)


def _sgu_out_kernel(x_ref, gate_ref, u_ref, v_ref, ws_ref, bs_ref, wo_ref, o_ref, gated_ref):
    tm = x_ref.shape[0]
    for ci in range(tm // CHUNK):
        rows = slice(ci * CHUNK, (ci + 1) * CHUNK)
        for g in range(SG_GROUPS):
            cols = slice(g * SG_GW, (g + 1) * SG_GW)
            vs = _dot(ws_ref[g], v_ref[rows, cols]) + bs_ref[:, cols]
            gated_ref[rows, cols] = (u_ref[rows, cols].astype(jnp.float32) * vs).astype(jnp.bfloat16)
    o_ref[...] = x_ref[...] + gate_ref[0] * _dot(gated_ref[...], wo_ref[...])


def _sgu_out(h, mods, uv, w_s, bs_full, w_out, layer, rows_per_batch, tm):
    m = h.shape[0]
    ms = functools.partial(_mod_spec, layer, tm=tm, rows_per_batch=rows_per_batch)
    return pl.pallas_call(
        _sgu_out_kernel,
        out_shape=jax.ShapeDtypeStruct((m, D), jnp.float32),
        grid=(m // tm,),
        in_specs=[
            pl.BlockSpec((tm, D), lambda i: (i, 0)),
            ms(2),
            pl.BlockSpec((tm, SG_HALF), lambda i: (i, 0)),
            pl.BlockSpec((tm, SG_HALF), lambda i: (i, 1)),
            pl.BlockSpec((SG_GROUPS, CHUNK, CHUNK), lambda i: (0, 0, 0)),
            pl.BlockSpec((CHUNK, SG_HALF), lambda i: (0, 0)),
            pl.BlockSpec((SG_HALF, D), lambda i: (0, 0)),
        ],
        out_specs=pl.BlockSpec((tm, D), lambda i: (i, 0)),
        scratch_shapes=[pltpu.VMEM((tm, SG_HALF), jnp.bfloat16)],
        compiler_params=_cparams(("parallel",)),
        name="sgu_out",
    )(h, mods, uv, uv, w_s, bs_full, w_out)


def _qkv_kernel(x_ref, g_ref, sh_ref, sc_ref, w_ref, hg_ref, o_ref, *, n_norm):
    hn = _norm_mod(x_ref[...], g_ref[0], sh_ref[0], sc_ref[0]).astype(jnp.bfloat16)
    lane = jax.lax.broadcasted_iota(jnp.int32, (1, 2 * NA_DH), 1)
    first = lane < NA_DH
    for n in range(w_ref.shape[1] // D):
        y = _dot(hn, w_ref[:, n * D:(n + 1) * D])
        if n >= n_norm:
            o_ref[:, n * D:(n + 1) * D] = y.astype(jnp.bfloat16)
            continue
        for j in range(D // (2 * NA_DH)):
            t = y[:, j * 2 * NA_DH:(j + 1) * 2 * NA_DH]
            cols = slice(n * D + j * 2 * NA_DH, n * D + (j + 1) * 2 * NA_DH)
            t2 = t * t
            s_all = jnp.sum(t2, axis=-1, keepdims=True)
            s_first = jnp.sum(jnp.where(first, t2, 0.0), axis=-1, keepdims=True)
            ms = jnp.where(first, s_first, s_all - s_first) * (1.0 / NA_DH)
            o_ref[:, cols] = (t * jax.lax.rsqrt(ms + EPS) * hg_ref[:, cols]).astype(jnp.bfloat16)


def _qkv(h, mods, norm_g, w, head_gains, layer, rows_per_batch, tm):
    m = h.shape[0]
    nd = w.shape[1]
    ms = functools.partial(_mod_spec, layer, tm=tm, rows_per_batch=rows_per_batch)
    return pl.pallas_call(
        functools.partial(_qkv_kernel, n_norm=nd // D - 1),
        out_shape=jax.ShapeDtypeStruct((m, nd), jnp.bfloat16),
        grid=(m // tm,),
        in_specs=[
            pl.BlockSpec((tm, D), lambda i: (i, 0)),
            pl.BlockSpec((1, 1, D), lambda i: (layer, 0, 0)),
            ms(0), ms(1),
            _resident((D, nd)),
            _resident((1, nd)),
        ],
        out_specs=pl.BlockSpec((tm, nd), lambda i: (i, 0)),
        compiler_params=_cparams(("parallel",)),
        name="na_qkv",
    )(h, norm_g, mods, mods, w, head_gains)


def _na_band_start(rb, rows):
    return np.clip(rb * NA_R - NA_KH // 2, 0, rows - NA_BAND)


def _na_bias_index(rows):
    idx = np.full((rows // NA_R, NA_R, NA_BAND), 2 * NA_KH - 1, np.int32)
    for rb in range(rows // NA_R):
        kb = _na_band_start(rb, rows)
        for i in range(NA_R):
            r = rb * NA_R + i
            rs = np.clip(r - NA_KH // 2, 0, rows - NA_KH)
            for j in range(NA_BAND):
                if rs <= kb + j < rs + NA_KH:
                    idx[rb, i, j] = kb + j - r + (NA_KH - 1)
    return idx.reshape(-1)


def _na_kernel(idx_ref, q_ref, k_ref, v_ref, kc_ref, vc_ref, bl_ref, br_ref, o_ref, s0_ref, s1_ref, p0_ref, p1_ref):
    g = pl.program_id(2)
    last = pl.num_programs(2) - 1
    tq = NA_R * GRID_W
    s_refs, p_refs = (s0_ref, s1_ref), (p0_ref, p1_ref)

    def scores(j):
        s_refs[j][...] = _na_scores(2 * g + j, q_ref[0, j * tq:(j + 1) * tq, :], k_ref, kc_ref[0])

    def probs(j):
        p_refs[1 - j][...] = _na_probs(2 * g + j - 1, idx_ref, s_refs[1 - j][...], bl_ref, br_ref)

    def values(j):
        o_ref[0, j * tq:(j + 1) * tq, :] = _na_values(2 * g + j - 2, p_refs[j][...], v_ref, vc_ref[0])

    @pl.when(g == 0)
    def _():
        scores(0)
        scores(1)
        probs(1)

    @pl.when(jnp.logical_and(g > 0, g < last))
    def _():
        for j in range(2):
            scores(j)
            probs(j)
            values(j)

    @pl.when(g == last)
    def _():
        probs(0)
        values(0)
        values(1)


def _na_band(rb, ref, tail):
    rows = ref.shape[1] // GRID_W
    kb = jnp.clip(rb * NA_R - NA_KH // 2, 0, rows - NA_BAND)
    start = pl.multiple_of(kb * GRID_W, GRID_W)
    return jnp.concatenate([ref[0, pl.ds(start, NA_BAND * GRID_W), :], tail], axis=0)


def _na_head_lanes():
    first = jax.lax.broadcasted_iota(jnp.int32, (1, 2 * NA_DH), 1) < NA_DH
    return first, jnp.logical_not(first)


def _na_scores(rb, q, k_ref, kc):
    qm = jnp.concatenate([jnp.where(keep, q, jnp.zeros_like(q)) for keep in _na_head_lanes()], axis=0)
    return jax.lax.dot_general(qm, _na_band(rb, k_ref, kc), (((1,), (1,)), ((), ())),
                               preferred_element_type=jnp.float32)


def _na_probs(rb, idx_ref, s, bl_ref, br_ref):
    tq = s.shape[0] // 2
    n_loc = NA_BAND * GRID_W
    out = []
    for hh in range(2):
        bias_rows = []
        for i in range(NA_R):
            base = (rb * NA_R + i) * NA_BAND
            pieces = [bl_ref[hh, idx_ref[base + 2 * jp]] + br_ref[hh, idx_ref[base + 2 * jp + 1]]
                      for jp in range(NA_BAND // 2)]
            bias_rows.append(jnp.concatenate(pieces, axis=1))
        s_loc = s[hh * tq:(hh + 1) * tq, :n_loc] + jnp.concatenate(bias_rows, axis=0)
        s_ctx = s[hh * tq:(hh + 1) * tq, n_loc:]
        mx = jnp.maximum(jnp.max(s_loc, axis=-1, keepdims=True), jnp.max(s_ctx, axis=-1, keepdims=True))
        out.append(jnp.concatenate([jnp.exp(s_loc - mx), jnp.exp(s_ctx - mx)], axis=1).astype(jnp.bfloat16))
    return jnp.concatenate(out, axis=0)


def _na_values(rb, p, v_ref, vc):
    tq = p.shape[0] // 2
    first, second = _na_head_lanes()
    vcat = _na_band(rb, v_ref, vc)
    one = jnp.ones((), jnp.bfloat16)
    outs = []
    for hh, keep in enumerate((first, second)):
        o = _dot(p[hh * tq:(hh + 1) * tq], jnp.where(keep, vcat, one))
        outs.append(o / pltpu.roll(o, NA_DH, axis=1))
    return jnp.where(first, outs[0], outs[1]).astype(jnp.bfloat16)


def _na_bias_tables(rpb):
    cols = np.arange(GRID_W)
    cs = np.clip(cols - NA_KW // 2, 0, GRID_W - NA_KW)
    rel = cols[None, :] - cols[:, None] + (NA_KW - 1)
    ok = (cols[None, :] >= cs[:, None]) & (cols[None, :] < cs[:, None] + NA_KW)
    dense = jnp.where(ok[None, None], rpb[:, :, np.clip(rel, 0, 2 * NA_KW - 2)], NEG)
    dense = jnp.concatenate([dense, jnp.full((NA_HEADS, 1, GRID_W, GRID_W), NEG, jnp.float32)], axis=1)
    zero = jnp.zeros_like(dense)
    return jnp.concatenate([dense, zero], axis=-1), jnp.concatenate([zero, dense], axis=-1)


def _na_attention(qkv, kvc, bias_l, bias_r, bsz, seq, ctx_len):
    hp = NA_HEADS // 2
    bw = 2 * NA_DH
    tq = 2 * NA_R * GRID_W
    steps = seq // tq
    n_keys = NA_BAND * GRID_W + ctx_len
    grid_spec = pltpu.PrefetchScalarGridSpec(
        num_scalar_prefetch=1,
        grid=(bsz, hp, steps + 1),
        in_specs=[
            pl.BlockSpec((1, tq, bw), lambda b, h, r, idx: (b, jnp.minimum(r, steps - 1), h)),
            pl.BlockSpec((1, seq, bw), lambda b, h, r, idx: (b, 0, hp + h)),
            pl.BlockSpec((1, seq, bw), lambda b, h, r, idx: (b, 0, 2 * hp + h)),
            pl.BlockSpec((1, ctx_len, bw), lambda b, h, r, idx: (b, 0, h)),
            pl.BlockSpec((1, ctx_len, bw), lambda b, h, r, idx: (b, 0, hp + h)),
            pl.BlockSpec((2, 2 * NA_KH, GRID_W, 2 * GRID_W), lambda b, h, r, idx: (h, 0, 0, 0)),
            pl.BlockSpec((2, 2 * NA_KH, GRID_W, 2 * GRID_W), lambda b, h, r, idx: (h, 0, 0, 0)),
        ],
        out_specs=pl.BlockSpec((1, tq, bw), lambda b, h, r, idx: (b, jnp.maximum(r - 1, 0), h)),
        scratch_shapes=[pltpu.VMEM((2 * NA_R * GRID_W, n_keys), dt)
                        for dt in (jnp.float32, jnp.float32, jnp.bfloat16, jnp.bfloat16)],
    )
    return pl.pallas_call(
        _na_kernel,
        out_shape=jax.ShapeDtypeStruct((bsz, seq, D), jnp.bfloat16),
        grid_spec=grid_spec,
        compiler_params=_cparams(("parallel", "parallel", "arbitrary")),
        name="na_attention",
    )(jnp.asarray(_na_bias_index(seq // GRID_W)), qkv, qkv, qkv, kvc, kvc, bias_l, bias_r)


def _proj_res_kernel(x_ref, gate_ref, a_ref, w_ref, o_ref):
    o_ref[...] = x_ref[...] + gate_ref[0] * _dot(a_ref[...], w_ref[...])


def _proj_res(h, mods, a, w, layer, rows_per_batch, tm):
    m = h.shape[0]
    ms = functools.partial(_mod_spec, layer, tm=tm, rows_per_batch=rows_per_batch)
    return pl.pallas_call(
        _proj_res_kernel,
        out_shape=jax.ShapeDtypeStruct((m, D), jnp.float32),
        grid=(m // tm,),
        in_specs=[
            pl.BlockSpec((tm, D), lambda i: (i, 0)),
            ms(2),
            pl.BlockSpec((tm, D), lambda i: (i, 0)),
            pl.BlockSpec((D, D), lambda i: (0, 0)),
        ],
        out_specs=pl.BlockSpec((tm, D), lambda i: (i, 0)),
        compiler_params=_cparams(("parallel",)),
        name="na_out_proj",
    )(h, mods, a, w)


def _glu_kernel(x_ref, g_ref, sh_ref, sc_ref, w_ref, b_ref, o_ref):
    hn = _norm_mod(x_ref[...], g_ref[0], sh_ref[0], sc_ref[0]).astype(jnp.bfloat16)
    y = _dot(hn, w_ref[...]) + b_ref[...]
    o_ref[...] = (y[:, :D] * jax.nn.sigmoid(y[:, D:])).astype(jnp.bfloat16)


def _glu(h, mods, norm_g, w, b, layer, rows_per_batch, tm):
    m = h.shape[0]
    ms = functools.partial(_mod_spec, layer, tm=tm, rows_per_batch=rows_per_batch)
    return pl.pallas_call(
        _glu_kernel,
        out_shape=jax.ShapeDtypeStruct((m, D), jnp.bfloat16),
        grid=(m // tm,),
        in_specs=[
            pl.BlockSpec((tm, D), lambda i: (i, 0)),
            pl.BlockSpec((1, 1, D), lambda i: (layer, 0, 0)),
            ms(0), ms(1),
            pl.BlockSpec((D, 2 * D), lambda i: (0, 0)),
            pl.BlockSpec((1, 2 * D), lambda i: (0, 0)),
        ],
        out_specs=pl.BlockSpec((tm, D), lambda i: (i, 0)),
        compiler_params=_cparams(("parallel",)),
        name="conv_glu",
    )(h, norm_g, mods, mods, w, b)


def _conv_kernel(x_ref, gate_ref, a_ref, prev_ref, next_ref, wd_ref, bd_ref, lg_ref, lb_ref,
                 w2_ref, b2_ref, o_ref, win_ref, conv_ref, *, tiles_per_seq):
    tm = x_ref.shape[0]
    i = pl.program_id(0)
    pos = i % tiles_per_seq
    prev = jnp.where(pos == 0, 0.0, prev_ref[...].astype(jnp.float32))
    nxt = jnp.where(pos == tiles_per_seq - 1, 0.0, next_ref[...].astype(jnp.float32))
    a = a_ref[...].astype(jnp.float32)
    n_slab = D // LANES
    for s in range(n_slab):
        cols = slice(s * LANES, (s + 1) * LANES)
        win_ref[s, 0:HALO, :] = prev[:, cols]
        win_ref[s, HALO:HALO + tm, :] = a[:, cols]
        win_ref[s, HALO + tm:, :] = nxt[:, cols]
    blk = PHASES * SUBLANES
    for s in range(n_slab):
        cols = slice(s * LANES, (s + 1) * LANES)
        taps = [jnp.broadcast_to(wd_ref[k:k + 1, cols], (SUBLANES, LANES)) for k in range(CONV_W)]

        def block(rb, carry, s=s, taps=taps):
            base = pl.multiple_of(rb * (CONV_UNROLL * blk), CONV_UNROLL * blk)
            for q in range(CONV_UNROLL * PHASES):
                row = (q // PHASES) * blk + q % PHASES
                acc = None
                for k in range(CONV_W):
                    off = row + k + HALO - CONV_PAD
                    term = win_ref[s, pl.ds(base + off, SUBLANES, stride=PHASES), :] * taps[k]
                    acc = term if acc is None else acc + term
                conv_ref[s, pl.ds(base + row, SUBLANES, stride=PHASES), :] = acc
            return carry

        jax.lax.fori_loop(0, tm // (CONV_UNROLL * blk), block, 0)
    conv = jnp.concatenate([conv_ref[s] for s in range(n_slab)], axis=1)
    y = _layer_norm(conv + bd_ref[...], lg_ref[...], lb_ref[...])
    y = (y * jax.nn.sigmoid(y)).astype(jnp.bfloat16)
    o_ref[...] = x_ref[...] + gate_ref[0] * (_dot(y, w2_ref[...]) + b2_ref[...])


def _conv_mix(h, mods, a, w_dw, b_dw, ln_g, ln_b, w_pw2, b_pw2, layer, rows_per_batch, tm):
    m = h.shape[0]
    ms = functools.partial(_mod_spec, layer, tm=tm, rows_per_batch=rows_per_batch)
    hb = tm // HALO
    last = m // HALO - 1
    vec = pl.BlockSpec((1, D), lambda i: (0, 0))
    return pl.pallas_call(
        functools.partial(_conv_kernel, tiles_per_seq=rows_per_batch // tm),
        out_shape=jax.ShapeDtypeStruct((m, D), jnp.float32),
        grid=(m // tm,),
        in_specs=[
            pl.BlockSpec((tm, D), lambda i: (i, 0)),
            ms(2),
            pl.BlockSpec((tm, D), lambda i: (i, 0)),
            pl.BlockSpec((HALO, D), lambda i: (jnp.maximum(i * hb - 1, 0), 0)),
            pl.BlockSpec((HALO, D), lambda i: (jnp.minimum((i + 1) * hb, last), 0)),
            pl.BlockSpec((CONV_W, D), lambda i: (0, 0)),
            vec, vec, vec,
            pl.BlockSpec((D, D), lambda i: (0, 0)),
            vec,
        ],
        out_specs=pl.BlockSpec((tm, D), lambda i: (i, 0)),
        scratch_shapes=[pltpu.VMEM((D // LANES, tm + 2 * HALO, LANES), jnp.float32),
                        pltpu.VMEM((D // LANES, tm, LANES), jnp.float32)],
        compiler_params=_cparams(("parallel",)),
        name="conv_mix",
    )(h, mods, a, a, a, w_dw, b_dw, ln_g, ln_b, w_pw2, b_pw2)


def kernel(x, c, ctx, c_ctx, ada_w, ada_b, norm1_g, norm2_g, a_w_in, a_b_in, a_ln_g, a_ln_b, a_w_s, a_b_s, a_w_out, b_w_qkv, b_w_o, b_q_g, b_k_g, b_rpb, c_w_pw1, c_b_pw1, c_w_dw, c_b_dw, c_ln_g, c_ln_b, c_w_pw2, c_b_pw2, mlp_w1, mlp_w2):
    bsz, seq, _ = x.shape
    ctx_len = ctx.shape[1]
    bf = jnp.bfloat16
    assert bsz <= CTX_ROW and seq % (2 * NA_R * GRID_W) == 0 and seq // GRID_W >= NA_BAND

    cond = jnp.zeros((MOD_ROWS, D), jnp.float32).at[:bsz].set(c).at[CTX_ROW].set(c_ctx)
    mods = _ada_mods(cond, ada_w, ada_b)
    n1 = norm1_g.reshape(DEPTH, 1, D)
    n2 = norm2_g.reshape(DEPTH, 1, D)

    h = x.reshape(bsz * seq, D)
    hc = ctx.reshape(bsz * ctx_len, D)
    tm, tmc = 512, 256

    for l in range(DEPTH):
        kind, j = l % N_MIXERS, l // N_MIXERS
        w1, w2 = mlp_w1[l].astype(bf), mlp_w2[l].astype(bf)
        if kind == 0:
            w_in, w_out, w_s = a_w_in[j].astype(bf), a_w_out[j].astype(bf), a_w_s[j].astype(bf)
            b_in = a_b_in[j].reshape(1, -1)
            ln_g, ln_b = a_ln_g[j].reshape(1, -1), a_ln_b[j].reshape(1, -1)
            bs_full = jnp.repeat(a_b_s[j].T, SG_GW, axis=1)
            uv = _sgu_in(h, mods, n1, w_in, b_in, ln_g, ln_b, l, seq, tm)
            h = _sgu_out(h, mods, uv, w_s, bs_full, w_out, l, seq, 512)
            if l < LAST_CTX_READER:
                uvc = _sgu_in(hc, mods, n1, w_in, b_in, ln_g, ln_b, l, None, tmc)
                hc = _sgu_out(hc, mods, uvc, w_s, bs_full, w_out, l, None, tmc)
                hc = _mlp(hc, mods, n2, w1, w2, l, None, tmc)
        elif kind == 1:
            w_qkv, w_o = b_w_qkv[j].astype(bf), b_w_o[j].astype(bf)
            gains = jnp.concatenate([jnp.tile(b_q_g[j], NA_HEADS) * NA_DH ** -0.5,
                                     jnp.tile(b_k_g[j], NA_HEADS),
                                     jnp.ones((D,), jnp.float32)]).reshape(1, 3 * D)
            bias_l, bias_r = _na_bias_tables(b_rpb[j])
            qkv = _qkv(h, mods, n1, w_qkv, gains, l, seq, tm)
            kvc = _qkv(hc, mods, n1, w_qkv[:, D:], gains[:, D:], l, None, tmc)
            o = _na_attention(qkv.reshape(bsz, seq, 3 * D), kvc.reshape(bsz, ctx_len, 2 * D),
                              bias_l, bias_r, bsz, seq, ctx_len)
            h = _proj_res(h, mods, o.reshape(bsz * seq, D), w_o, l, seq, tm)
        else:
            a = _glu(h, mods, n1, c_w_pw1[j].astype(bf), c_b_pw1[j].reshape(1, -1), l, seq, tm)
            h = _conv_mix(h, mods, a, c_w_dw[j], c_b_dw[j].reshape(1, -1), c_ln_g[j].reshape(1, -1),
                          c_ln_b[j].reshape(1, -1), c_w_pw2[j].astype(bf), c_b_pw2[j].reshape(1, -1),
                          l, seq, 256)
        h = _mlp(h, mods, n2, w1, w2, l, seq, tm)
    return h.reshape(bsz, seq, D)
```

```python
import functools

import jax
import jax.numpy as jnp
import numpy as np
from jax.experimental import pallas as pl
from jax.experimental.pallas import tpu as pltpu

D = 1024
DEPTH = 4
GRID_W = 64
N_MIXERS = 3
CHUNK = 128
SG_GROUPS = 8
SG_HALF = 2 * D
SG_GW = SG_HALF // SG_GROUPS
SG_IN_CHUNK = 1024
NA_HEADS = 16
NA_DH = D // NA_HEADS
NA_KH = 8
NA_KW = 16
CONV_W = 31
CONV_PAD = CONV_W // 2
D_FF = 4 * D
MLP_TF = 1024
EPS = 1e-6
LAST_CTX_READER = max(range(1, DEPTH, N_MIXERS))

MOD_ROWS = 16
CTX_ROW = 8
NEG = -1e30
VMEM_LIMIT = 56 * 1024 * 1024

NA_R = 4
NA_BAND = 12
HALO = 16
SUBLANES, LANES = 8, 128
PHASES = 4
CONV_UNROLL = 2


def _cparams(sem):
    return pltpu.CompilerParams(dimension_semantics=sem, vmem_limit_bytes=VMEM_LIMIT)


def _norm_mod(x, g, sh, sc):
    ms = jnp.mean(x * x, axis=-1, keepdims=True)
    return x * jax.lax.rsqrt(ms + EPS) * (g * (1.0 + sc)) + sh


def _gelu_tanh(x):
    c = -2.0 * np.sqrt(2.0 / np.pi) * np.log2(np.e)
    return x / (1.0 + jnp.exp2(x * (c + (0.044715 * c) * (x * x))))


def _layer_norm(x, g, b):
    mu = jnp.mean(x, axis=-1, keepdims=True)
    xc = x - mu
    var = jnp.mean(xc * xc, axis=-1, keepdims=True)
    return xc * jax.lax.rsqrt(var + EPS) * g + b


def _dot(a, b):
    return jnp.dot(a, b, preferred_element_type=jnp.float32)


def _ada_kernel(c_ref, w_ref, b_ref, o_ref):
    cond = c_ref[...]
    act = (cond * jax.nn.sigmoid(cond)).astype(jnp.bfloat16)
    o_ref[0] = _dot(act, w_ref[0].astype(jnp.bfloat16)) + b_ref[0]


def _ada_mods(cond, ada_w, ada_b):
    tn = 1536
    out = pl.pallas_call(
        _ada_kernel,
        out_shape=jax.ShapeDtypeStruct((DEPTH, MOD_ROWS, 6 * D), jnp.float32),
        grid=(DEPTH, 6 * D // tn),
        in_specs=[
            pl.BlockSpec((MOD_ROWS, D), lambda l, n: (0, 0)),
            pl.BlockSpec((1, D, tn), lambda l, n: (l, 0, n)),
            pl.BlockSpec((1, 1, tn), lambda l, n: (l, 0, n)),
        ],
        out_specs=pl.BlockSpec((1, MOD_ROWS, tn), lambda l, n: (l, 0, n)),
        compiler_params=_cparams(("arbitrary", "arbitrary")),
        name="ada_mods",
    )(cond, ada_w, ada_b.reshape(DEPTH, 1, 6 * D))
    return out.reshape(DEPTH * MOD_ROWS, 1, 6 * D)


def _mod_spec(layer, chunk, tm, rows_per_batch):
    if rows_per_batch is None:
        return pl.BlockSpec((1, 1, D), lambda i, *_: (layer * MOD_ROWS + CTX_ROW, 0, chunk))
    return pl.BlockSpec(
        (1, 1, D), lambda i, *_: (layer * MOD_ROWS + (i * tm) // rows_per_batch, 0, chunk))


def _mlp_tail(h1, g_ref, sh_ref, sc_ref, gate_ref, w1_ref, w2_ref):
    hn = _norm_mod(h1, g_ref[0], sh_ref[0], sc_ref[0]).astype(jnp.bfloat16)
    acc = None
    for f in range(D_FF // MLP_TF):
        a = jnp.maximum(_dot(hn, w1_ref[:, f * MLP_TF:(f + 1) * MLP_TF]), 0.0)
        part = _dot((a * a).astype(jnp.bfloat16), w2_ref[f * MLP_TF:(f + 1) * MLP_TF, :])
        acc = part if acc is None else acc + part
    return h1 + gate_ref[0] * acc


def _resident(shape):
    return pl.BlockSpec(shape, lambda i: (0,) * len(shape), pipeline_mode=pl.Buffered(1))


def _mlp_specs(layer, ms):
    return [pl.BlockSpec((1, 1, D), lambda i: (layer, 0, 0)), ms(3), ms(4), ms(5),
            _resident((D, D_FF)), _resident((D_FF, D))]


def _sgu_in_kernel(x_ref, g_ref, sh_ref, sc_ref, w_ref, b_ref, lg_ref, lb_ref, o_ref):
    hn = _norm_mod(x_ref[...], g_ref[0], sh_ref[0], sc_ref[0]).astype(jnp.bfloat16)
    tn = SG_IN_CHUNK

    def act(c):
        return _gelu_tanh(_dot(hn, w_ref[:, c * tn:(c + 1) * tn]) + b_ref[:, c * tn:(c + 1) * tn])

    vs = [act(c) for c in range(SG_HALF // tn, 2 * SG_HALF // tn)]
    mu = sum(jnp.sum(v, axis=-1, keepdims=True) for v in vs) * (1.0 / SG_HALF)
    vs = [v - mu for v in vs]
    var = sum(jnp.sum(v * v, axis=-1, keepdims=True) for v in vs) * (1.0 / SG_HALF)
    rstd = jax.lax.rsqrt(var + EPS)
    for c, v in enumerate(vs):
        cols = slice(c * tn, (c + 1) * tn)
        o_ref[:, SG_HALF + c * tn:SG_HALF + (c + 1) * tn] = (
            v * rstd * lg_ref[:, cols] + lb_ref[:, cols]).astype(jnp.bfloat16)
    for c in range(SG_HALF // tn):
        o_ref[:, c * tn:(c + 1) * tn] = act(c).astype(jnp.bfloat16)


def _sgu_in(h, mods, norm_g, w_in, b_in, ln_g, ln_b, layer, rows_per_batch, tm):
    m = h.shape[0]
    ms = functools.partial(_mod_spec, layer, tm=tm, rows_per_batch=rows_per_batch)
    return pl.pallas_call(
        _sgu_in_kernel,
        out_shape=jax.ShapeDtypeStruct((m, 2 * SG_HALF), jnp.bfloat16),
        grid=(m // tm,),
        in_specs=[
            pl.BlockSpec((tm, D), lambda i: (i, 0)),
            pl.BlockSpec((1, 1, D), lambda i: (layer, 0, 0)),
            ms(0), ms(1),
            _resident((D, 2 * SG_HALF)),
            _resident((1, 2 * SG_HALF)),
            _resident((1, SG_HALF)),
            _resident((1, SG_HALF)),
        ],
        out_specs=pl.BlockSpec((tm, 2 * SG_HALF), lambda i: (i, 0)),
        compiler_params=_cparams(("parallel",)),
        name="sgu_in",
    )(h, norm_g, mods, mods, w_in, b_in, ln_g, ln_b)


def _sgu_out_kernel(x_ref, gate_ref, u_ref, v_ref, ws_ref, bs_ref, wo_ref, *rest):
    *mlp_refs, o_ref, gated_ref = rest
    tm = x_ref.shape[0]
    for ci in range(tm // CHUNK):
        rows = slice(ci * CHUNK, (ci + 1) * CHUNK)
        for g in range(SG_GROUPS):
            cols = slice(g * SG_GW, (g + 1) * SG_GW)
            vs = _dot(ws_ref[g], v_ref[rows, cols]) + bs_ref[:, cols]
            gated_ref[rows, cols] = (u_ref[rows, cols].astype(jnp.float32) * vs).astype(jnp.bfloat16)
    h1 = x_ref[...] + gate_ref[0] * _dot(gated_ref[...], wo_ref[...])
    o_ref[...] = _mlp_tail(h1, *mlp_refs)


def _sgu_out(h, mods, uv, w_s, bs_full, w_out, norm_g, w1, w2, layer, rows_per_batch, tm):
    m = h.shape[0]
    ms = functools.partial(_mod_spec, layer, tm=tm, rows_per_batch=rows_per_batch)
    return pl.pallas_call(
        _sgu_out_kernel,
        out_shape=jax.ShapeDtypeStruct((m, D), jnp.float32),
        grid=(m // tm,),
        in_specs=[
            pl.BlockSpec((tm, D), lambda i: (i, 0)),
            ms(2),
            pl.BlockSpec((tm, SG_HALF), lambda i: (i, 0)),
            pl.BlockSpec((tm, SG_HALF), lambda i: (i, 1)),
            _resident((SG_GROUPS, CHUNK, CHUNK)),
            _resident((CHUNK, SG_HALF)),
            _resident((SG_HALF, D)),
        ] + _mlp_specs(layer, ms),
        out_specs=pl.BlockSpec((tm, D), lambda i: (i, 0)),
        scratch_shapes=[pltpu.VMEM((tm, SG_HALF), jnp.bfloat16)],
        compiler_params=_cparams(("parallel",)),
        name="sgu_out_mlp",
    )(h, mods, uv, uv, w_s, bs_full, w_out, norm_g, mods, mods, mods, w1, w2)


def _qkv_kernel(x_ref, g_ref, sh_ref, sc_ref, w_ref, hg_ref, o_ref, *, n_norm):
    hn = _norm_mod(x_ref[...], g_ref[0], sh_ref[0], sc_ref[0]).astype(jnp.bfloat16)
    lane = jax.lax.broadcasted_iota(jnp.int32, (1, 2 * NA_DH), 1)
    first = lane < NA_DH
    for n in range(w_ref.shape[1] // D):
        y = _dot(hn, w_ref[:, n * D:(n + 1) * D])
        if n >= n_norm:
            o_ref[:, n * D:(n + 1) * D] = y.astype(jnp.bfloat16)
            continue
        for j in range(D // (2 * NA_DH)):
            t = y[:, j * 2 * NA_DH:(j + 1) * 2 * NA_DH]
            cols = slice(n * D + j * 2 * NA_DH, n * D + (j + 1) * 2 * NA_DH)
            t2 = t * t
            s_all = jnp.sum(t2, axis=-1, keepdims=True)
            s_first = jnp.sum(jnp.where(first, t2, 0.0), axis=-1, keepdims=True)
            ms = jnp.where(first, s_first, s_all - s_first) * (1.0 / NA_DH)
            o_ref[:, cols] = (t * jax.lax.rsqrt(ms + EPS) * hg_ref[:, cols]).astype(jnp.bfloat16)


def _qkv(h, mods, norm_g, w, head_gains, layer, rows_per_batch, tm):
    m = h.shape[0]
    nd = w.shape[1]
    ms = functools.partial(_mod_spec, layer, tm=tm, rows_per_batch=rows_per_batch)
    return pl.pallas_call(
        functools.partial(_qkv_kernel, n_norm=nd // D - 1),
        out_shape=jax.ShapeDtypeStruct((m, nd), jnp.bfloat16),
        grid=(m // tm,),
        in_specs=[
            pl.BlockSpec((tm, D), lambda i: (i, 0)),
            pl.BlockSpec((1, 1, D), lambda i: (layer, 0, 0)),
            ms(0), ms(1),
            _resident((D, nd)),
            _resident((1, nd)),
        ],
        out_specs=pl.BlockSpec((tm, nd), lambda i: (i, 0)),
        compiler_params=_cparams(("parallel",)),
        name="na_qkv",
    )(h, norm_g, mods, mods, w, head_gains)


def _na_band_start(rb, rows):
    return np.clip(rb * NA_R - NA_KH // 2, 0, rows - NA_BAND)


def _na_bias_index(rows):
    idx = np.full((rows // NA_R, NA_R, NA_BAND), 2 * NA_KH - 1, np.int32)
    for rb in range(rows // NA_R):
        kb = _na_band_start(rb, rows)
        for i in range(NA_R):
            r = rb * NA_R + i
            rs = np.clip(r - NA_KH // 2, 0, rows - NA_KH)
            for j in range(NA_BAND):
                if rs <= kb + j < rs + NA_KH:
                    idx[rb, i, j] = kb + j - r + (NA_KH - 1)
    return idx.reshape(-1)


def _na_kernel(idx_ref, q_ref, k_ref, v_ref, kc_ref, vc_ref, bl_ref, br_ref, o_ref, s0_ref, s1_ref, p0_ref, p1_ref):
    g = pl.program_id(2)
    last = pl.num_programs(2) - 1
    tq = NA_R * GRID_W
    s_refs, p_refs = (s0_ref, s1_ref), (p0_ref, p1_ref)

    def scores(j):
        s_refs[j][...] = _na_scores(2 * g + j, q_ref[0, j * tq:(j + 1) * tq, :], k_ref, kc_ref[0])

    def probs(j):
        p_refs[1 - j][...] = _na_probs(2 * g + j - 1, idx_ref, s_refs[1 - j][...], bl_ref, br_ref)

    def values(j):
        o_ref[0, j * tq:(j + 1) * tq, :] = _na_values(2 * g + j - 2, p_refs[j][...], v_ref, vc_ref[0])

    @pl.when(g == 0)
    def _():
        scores(0)
        scores(1)
        probs(1)

    @pl.when(jnp.logical_and(g > 0, g < last))
    def _():
        for j in range(2):
            scores(j)
            probs(j)
            values(j)

    @pl.when(g == last)
    def _():
        probs(0)
        values(0)
        values(1)


def _na_band(rb, ref, tail):
    rows = ref.shape[1] // GRID_W
    kb = jnp.clip(rb * NA_R - NA_KH // 2, 0, rows - NA_BAND)
    start = pl.multiple_of(kb * GRID_W, GRID_W)
    return jnp.concatenate([ref[0, pl.ds(start, NA_BAND * GRID_W), :], tail], axis=0)


def _na_head_lanes():
    first = jax.lax.broadcasted_iota(jnp.int32, (1, 2 * NA_DH), 1) < NA_DH
    return first, jnp.logical_not(first)


def _na_scores(rb, q, k_ref, kc):
    qm = jnp.concatenate([jnp.where(keep, q, jnp.zeros_like(q)) for keep in _na_head_lanes()], axis=0)
    return jax.lax.dot_general(qm, _na_band(rb, k_ref, kc), (((1,), (1,)), ((), ())),
                               preferred_element_type=jnp.float32)


def _na_probs(rb, idx_ref, s, bl_ref, br_ref):
    tq = s.shape[0] // 2
    n_loc = NA_BAND * GRID_W
    out = []
    for hh in range(2):
        bias_rows = []
        for i in range(NA_R):
            base = (rb * NA_R + i) * NA_BAND
            pieces = [bl_ref[hh, idx_ref[base + 2 * jp]] + br_ref[hh, idx_ref[base + 2 * jp + 1]]
                      for jp in range(NA_BAND // 2)]
            bias_rows.append(jnp.concatenate(pieces, axis=1))
        s_loc = s[hh * tq:(hh + 1) * tq, :n_loc] + jnp.concatenate(bias_rows, axis=0)
        s_ctx = s[hh * tq:(hh + 1) * tq, n_loc:]
        mx = jnp.maximum(jnp.max(s_loc, axis=-1, keepdims=True), jnp.max(s_ctx, axis=-1, keepdims=True))
        out.append(jnp.concatenate([jnp.exp(s_loc - mx), jnp.exp(s_ctx - mx)], axis=1).astype(jnp.bfloat16))
    return jnp.concatenate(out, axis=0)


def _na_values(rb, p, v_ref, vc):
    tq = p.shape[0] // 2
    first, second = _na_head_lanes()
    vcat = _na_band(rb, v_ref, vc)
    one = jnp.ones((), jnp.bfloat16)
    outs = []
    for hh, keep in enumerate((first, second)):
        o = _dot(p[hh * tq:(hh + 1) * tq], jnp.where(keep, vcat, one))
        outs.append(o / pltpu.roll(o, NA_DH, axis=1))
    return jnp.where(first, outs[0], outs[1]).astype(jnp.bfloat16)


def _na_bias_tables(rpb):
    cols = np.arange(GRID_W)
    cs = np.clip(cols - NA_KW // 2, 0, GRID_W - NA_KW)
    rel = cols[None, :] - cols[:, None] + (NA_KW - 1)
    ok = (cols[None, :] >= cs[:, None]) & (cols[None, :] < cs[:, None] + NA_KW)
    dense = jnp.where(ok[None, None], rpb[:, :, np.clip(rel, 0, 2 * NA_KW - 2)], NEG)
    dense = jnp.concatenate([dense, jnp.full((NA_HEADS, 1, GRID_W, GRID_W), NEG, jnp.float32)], axis=1)
    zero = jnp.zeros_like(dense)
    return jnp.concatenate([dense, zero], axis=-1), jnp.concatenate([zero, dense], axis=-1)


def _na_attention(qkv, kvc, bias_l, bias_r, bsz, seq, ctx_len):
    hp = NA_HEADS // 2
    bw = 2 * NA_DH
    tq = 2 * NA_R * GRID_W
    steps = seq // tq
    n_keys = NA_BAND * GRID_W + ctx_len
    grid_spec = pltpu.PrefetchScalarGridSpec(
        num_scalar_prefetch=1,
        grid=(bsz, hp, steps + 1),
        in_specs=[
            pl.BlockSpec((1, tq, bw), lambda b, h, r, idx: (b, jnp.minimum(r, steps - 1), h)),
            pl.BlockSpec((1, seq, bw), lambda b, h, r, idx: (b, 0, hp + h)),
            pl.BlockSpec((1, seq, bw), lambda b, h, r, idx: (b, 0, 2 * hp + h)),
            pl.BlockSpec((1, ctx_len, bw), lambda b, h, r, idx: (b, 0, h)),
            pl.BlockSpec((1, ctx_len, bw), lambda b, h, r, idx: (b, 0, hp + h)),
            pl.BlockSpec((2, 2 * NA_KH, GRID_W, 2 * GRID_W), lambda b, h, r, idx: (h, 0, 0, 0)),
            pl.BlockSpec((2, 2 * NA_KH, GRID_W, 2 * GRID_W), lambda b, h, r, idx: (h, 0, 0, 0)),
        ],
        out_specs=pl.BlockSpec((1, tq, bw), lambda b, h, r, idx: (b, jnp.maximum(r - 1, 0), h)),
        scratch_shapes=[pltpu.VMEM((2 * NA_R * GRID_W, n_keys), dt)
                        for dt in (jnp.float32, jnp.float32, jnp.bfloat16, jnp.bfloat16)],
    )
    return pl.pallas_call(
        _na_kernel,
        out_shape=jax.ShapeDtypeStruct((bsz, seq, D), jnp.bfloat16),
        grid_spec=grid_spec,
        compiler_params=_cparams(("parallel", "parallel", "arbitrary")),
        name="na_attention",
    )(jnp.asarray(_na_bias_index(seq // GRID_W)), qkv, qkv, qkv, kvc, kvc, bias_l, bias_r)


def _proj_res_kernel(x_ref, gate_ref, a_ref, w_ref, *rest):
    *mlp_refs, o_ref = rest
    h1 = x_ref[...] + gate_ref[0] * _dot(a_ref[...], w_ref[...])
    o_ref[...] = _mlp_tail(h1, *mlp_refs)


def _proj_res(h, mods, a, w, norm_g, w1, w2, layer, rows_per_batch, tm):
    m = h.shape[0]
    ms = functools.partial(_mod_spec, layer, tm=tm, rows_per_batch=rows_per_batch)
    return pl.pallas_call(
        _proj_res_kernel,
        out_shape=jax.ShapeDtypeStruct((m, D), jnp.float32),
        grid=(m // tm,),
        in_specs=[
            pl.BlockSpec((tm, D), lambda i: (i, 0)),
            ms(2),
            pl.BlockSpec((tm, D), lambda i: (i, 0)),
            _resident((D, D)),
        ] + _mlp_specs(layer, ms),
        out_specs=pl.BlockSpec((tm, D), lambda i: (i, 0)),
        compiler_params=_cparams(("parallel",)),
        name="na_out_proj_mlp",
    )(h, mods, a, w, norm_g, mods, mods, mods, w1, w2)


def _glu_kernel(x_ref, g_ref, sh_ref, sc_ref, w_ref, b_ref, o_ref):
    hn = _norm_mod(x_ref[...], g_ref[0], sh_ref[0], sc_ref[0]).astype(jnp.bfloat16)
    y = _dot(hn, w_ref[...]) + b_ref[...]
    o_ref[...] = (y[:, :D] * jax.nn.sigmoid(y[:, D:])).astype(jnp.bfloat16)


def _glu(h, mods, norm_g, w, b, layer, rows_per_batch, tm):
    m = h.shape[0]
    ms = functools.partial(_mod_spec, layer, tm=tm, rows_per_batch=rows_per_batch)
    return pl.pallas_call(
        _glu_kernel,
        out_shape=jax.ShapeDtypeStruct((m, D), jnp.bfloat16),
        grid=(m // tm,),
        in_specs=[
            pl.BlockSpec((tm, D), lambda i: (i, 0)),
            pl.BlockSpec((1, 1, D), lambda i: (layer, 0, 0)),
            ms(0), ms(1),
            pl.BlockSpec((D, 2 * D), lambda i: (0, 0)),
            pl.BlockSpec((1, 2 * D), lambda i: (0, 0)),
        ],
        out_specs=pl.BlockSpec((tm, D), lambda i: (i, 0)),
        compiler_params=_cparams(("parallel",)),
        name="conv_glu",
    )(h, norm_g, mods, mods, w, b)


def _conv_kernel(x_ref, gate_ref, a_ref, prev_ref, next_ref, wd_ref, bd_ref, lg_ref, lb_ref,
                 w2_ref, b2_ref, *rest, tiles_per_seq):
    *mlp_refs, o_ref, win_ref, conv_ref = rest
    tm = x_ref.shape[0]
    i = pl.program_id(0)
    pos = i % tiles_per_seq
    prev = jnp.where(pos == 0, 0.0, prev_ref[...].astype(jnp.float32))
    nxt = jnp.where(pos == tiles_per_seq - 1, 0.0, next_ref[...].astype(jnp.float32))
    a = a_ref[...].astype(jnp.float32)
    n_slab = D // LANES
    for s in range(n_slab):
        cols = slice(s * LANES, (s + 1) * LANES)
        win_ref[s, 0:HALO, :] = prev[:, cols]
        win_ref[s, HALO:HALO + tm, :] = a[:, cols]
        win_ref[s, HALO + tm:, :] = nxt[:, cols]
    blk = PHASES * SUBLANES
    for s in range(n_slab):
        cols = slice(s * LANES, (s + 1) * LANES)
        taps = [jnp.broadcast_to(wd_ref[k:k + 1, cols], (SUBLANES, LANES)) for k in range(CONV_W)]

        def block(rb, carry, s=s, taps=taps):
            base = pl.multiple_of(rb * (CONV_UNROLL * blk), CONV_UNROLL * blk)
            for q in range(CONV_UNROLL * PHASES):
                row = (q // PHASES) * blk + q % PHASES
                acc = None
                for k in range(CONV_W):
                    off = row + k + HALO - CONV_PAD
                    term = win_ref[s, pl.ds(base + off, SUBLANES, stride=PHASES), :] * taps[k]
                    acc = term if acc is None else acc + term
                conv_ref[s, pl.ds(base + row, SUBLANES, stride=PHASES), :] = acc
            return carry

        jax.lax.fori_loop(0, tm // (CONV_UNROLL * blk), block, 0)
    conv = jnp.concatenate([conv_ref[s] for s in range(n_slab)], axis=1)
    y = _layer_norm(conv + bd_ref[...], lg_ref[...], lb_ref[...])
    y = (y * jax.nn.sigmoid(y)).astype(jnp.bfloat16)
    h1 = x_ref[...] + gate_ref[0] * (_dot(y, w2_ref[...]) + b2_ref[...])
    o_ref[...] = _mlp_tail(h1, *mlp_refs)


def _conv_mix(h, mods, a, w_dw, b_dw, ln_g, ln_b, w_pw2, b_pw2, norm_g, w1, w2, layer, rows_per_batch, tm):
    m = h.shape[0]
    ms = functools.partial(_mod_spec, layer, tm=tm, rows_per_batch=rows_per_batch)
    hb = tm // HALO
    last = m // HALO - 1
    vec = pl.BlockSpec((1, D), lambda i: (0, 0))
    return pl.pallas_call(
        functools.partial(_conv_kernel, tiles_per_seq=rows_per_batch // tm),
        out_shape=jax.ShapeDtypeStruct((m, D), jnp.float32),
        grid=(m // tm,),
        in_specs=[
            pl.BlockSpec((tm, D), lambda i: (i, 0)),
            ms(2),
            pl.BlockSpec((tm, D), lambda i: (i, 0)),
            pl.BlockSpec((HALO, D), lambda i: (jnp.maximum(i * hb - 1, 0), 0)),
            pl.BlockSpec((HALO, D), lambda i: (jnp.minimum((i + 1) * hb, last), 0)),
            pl.BlockSpec((CONV_W, D), lambda i: (0, 0)),
            vec, vec, vec,
            _resident((D, D)),
            vec,
        ] + _mlp_specs(layer, ms),
        out_specs=pl.BlockSpec((tm, D), lambda i: (i, 0)),
        scratch_shapes=[pltpu.VMEM((D // LANES, tm + 2 * HALO, LANES), jnp.float32),
                        pltpu.VMEM((D // LANES, tm, LANES), jnp.float32)],
        compiler_params=_cparams(("parallel",)),
        name="conv_mix_mlp",
    )(h, mods, a, a, a, w_dw, b_dw, ln_g, ln_b, w_pw2, b_pw2, norm_g, mods, mods, mods, w1, w2)


def kernel(x, c, ctx, c_ctx, ada_w, ada_b, norm1_g, norm2_g, a_w_in, a_b_in, a_ln_g, a_ln_b, a_w_s, a_b_s, a_w_out, b_w_qkv, b_w_o, b_q_g, b_k_g, b_rpb, c_w_pw1, c_b_pw1, c_w_dw, c_b_dw, c_ln_g, c_ln_b, c_w_pw2, c_b_pw2, mlp_w1, mlp_w2):
    bsz, seq, _ = x.shape
    ctx_len = ctx.shape[1]
    bf = jnp.bfloat16
    assert bsz <= CTX_ROW and seq % (2 * NA_R * GRID_W) == 0 and seq // GRID_W >= NA_BAND

    cond = jnp.zeros((MOD_ROWS, D), jnp.float32).at[:bsz].set(c).at[CTX_ROW].set(c_ctx)
    mods = _ada_mods(cond, ada_w, ada_b)
    n1 = norm1_g.reshape(DEPTH, 1, D)
    n2 = norm2_g.reshape(DEPTH, 1, D)

    h = x.reshape(bsz * seq, D)
    hc = ctx.reshape(bsz * ctx_len, D)
    tm, tmc = 512, 256

    for l in range(DEPTH):
        kind, j = l % N_MIXERS, l // N_MIXERS
        w1, w2 = mlp_w1[l].astype(bf), mlp_w2[l].astype(bf)
        if kind == 0:
            w_in, w_out, w_s = a_w_in[j].astype(bf), a_w_out[j].astype(bf), a_w_s[j].astype(bf)
            b_in = a_b_in[j].reshape(1, -1)
            ln_g, ln_b = a_ln_g[j].reshape(1, -1), a_ln_b[j].reshape(1, -1)
            bs_full = jnp.repeat(a_b_s[j].T, SG_GW, axis=1)
            uv = _sgu_in(h, mods, n1, w_in, b_in, ln_g, ln_b, l, seq, tm)
            h = _sgu_out(h, mods, uv, w_s, bs_full, w_out, n2, w1, w2, l, seq, tm)
            if l < LAST_CTX_READER:
                uvc = _sgu_in(hc, mods, n1, w_in, b_in, ln_g, ln_b, l, None, tmc)
                hc = _sgu_out(hc, mods, uvc, w_s, bs_full, w_out, n2, w1, w2, l, None, tmc)
        elif kind == 1:
            w_qkv, w_o = b_w_qkv[j].astype(bf), b_w_o[j].astype(bf)
            gains = jnp.concatenate([jnp.tile(b_q_g[j], NA_HEADS) * NA_DH ** -0.5,
                                     jnp.tile(b_k_g[j], NA_HEADS),
                                     jnp.ones((D,), jnp.float32)]).reshape(1, 3 * D)
            bias_l, bias_r = _na_bias_tables(b_rpb[j])
            qkv = _qkv(h, mods, n1, w_qkv, gains, l, seq, tm)
            kvc = _qkv(hc, mods, n1, w_qkv[:, D:], gains[:, D:], l, None, tmc)
            o = _na_attention(qkv.reshape(bsz, seq, 3 * D), kvc.reshape(bsz, ctx_len, 2 * D),
                              bias_l, bias_r, bsz, seq, ctx_len)
            h = _proj_res(h, mods, o.reshape(bsz * seq, D), w_o, n2, w1, w2, l, seq, tm)
        else:
            a = _glu(h, mods, n1, c_w_pw1[j].astype(bf), c_b_pw1[j].reshape(1, -1), l, seq, tm)
            h = _conv_mix(h, mods, a, c_w_dw[j], c_b_dw[j].reshape(1, -1), c_ln_g[j].reshape(1, -1),
                          c_ln_b[j].reshape(1, -1), c_w_pw2[j].astype(bf), c_b_pw2[j].reshape(1, -1),
                          n2, w1, w2, l, seq, 256)
    return h.reshape(bsz, seq, D)
```

```python
import functools

import jax
import jax.numpy as jnp
import numpy as np
from jax.experimental import pallas as pl
from jax.experimental.pallas import tpu as pltpu

D = 1024
DEPTH = 4
GRID_W = 64
N_MIXERS = 3
CHUNK = 128
SG_GROUPS = 8
SG_HALF = 2 * D
SG_GW = SG_HALF // SG_GROUPS
SG_IN_CHUNK = 1024
NA_HEADS = 16
NA_DH = D // NA_HEADS
NA_KH = 8
NA_KW = 16
CONV_W = 31
CONV_PAD = CONV_W // 2
D_FF = 4 * D
MLP_TF = 1024
EPS = 1e-6
LAST_CTX_READER = max(range(1, DEPTH, N_MIXERS))

MOD_ROWS = 16
CTX_ROW = 8
NEG = -1e30
VMEM_LIMIT = 56 * 1024 * 1024

NA_R = 4
NA_BAND = 12
HALO = 16
SUBLANES, LANES = 8, 128
PHASES = 4
CONV_UNROLL = 2


def _cparams(sem):
    return pltpu.CompilerParams(dimension_semantics=sem, vmem_limit_bytes=VMEM_LIMIT)


def _norm_mod(x, g, sh, sc):
    ms = jnp.mean(x * x, axis=-1, keepdims=True)
    return x * jax.lax.rsqrt(ms + EPS) * (g * (1.0 + sc)) + sh


def _gelu_tanh(x):
    c = -2.0 * np.sqrt(2.0 / np.pi) * np.log2(np.e)
    return x / (1.0 + jnp.exp2(x * (c + (0.044715 * c) * (x * x))))


def _layer_norm(x, g, b):
    mu = jnp.mean(x, axis=-1, keepdims=True)
    xc = x - mu
    var = jnp.mean(xc * xc, axis=-1, keepdims=True)
    return xc * jax.lax.rsqrt(var + EPS) * g + b


def _dot(a, b):
    return jnp.dot(a, b, preferred_element_type=jnp.float32)


def _ada_kernel(c_ref, w_ref, b_ref, o_ref):
    cond = c_ref[...]
    act = (cond * jax.nn.sigmoid(cond)).astype(jnp.bfloat16)
    o_ref[0] = _dot(act, w_ref[0].astype(jnp.bfloat16)) + b_ref[0]


def _ada_mods(cond, ada_w, ada_b):
    tn = 1536
    out = pl.pallas_call(
        _ada_kernel,
        out_shape=jax.ShapeDtypeStruct((DEPTH, MOD_ROWS, 6 * D), jnp.float32),
        grid=(DEPTH, 6 * D // tn),
        in_specs=[
            pl.BlockSpec((MOD_ROWS, D), lambda l, n: (0, 0)),
            pl.BlockSpec((1, D, tn), lambda l, n: (l, 0, n)),
            pl.BlockSpec((1, 1, tn), lambda l, n: (l, 0, n)),
        ],
        out_specs=pl.BlockSpec((1, MOD_ROWS, tn), lambda l, n: (l, 0, n)),
        compiler_params=_cparams(("arbitrary", "arbitrary")),
        name="ada_mods",
    )(cond, ada_w, ada_b.reshape(DEPTH, 1, 6 * D))
    return out.reshape(DEPTH * MOD_ROWS, 1, 6 * D)


def _mod_spec(layer, chunk, tm, rows_per_batch):
    if rows_per_batch is None:
        return pl.BlockSpec((1, 1, D), lambda i, *_: (layer * MOD_ROWS + CTX_ROW, 0, chunk))
    return pl.BlockSpec(
        (1, 1, D), lambda i, *_: (layer * MOD_ROWS + (i * tm) // rows_per_batch, 0, chunk))


def _mlp_tail(h1, g_ref, sh_ref, sc_ref, gate_ref, w1_ref, w2_ref):
    hn = _norm_mod(h1, g_ref[0], sh_ref[0], sc_ref[0]).astype(jnp.bfloat16)
    acc = None
    for f in range(D_FF // MLP_TF):
        a = jnp.maximum(_dot(hn, w1_ref[:, f * MLP_TF:(f + 1) * MLP_TF]), 0.0)
        part = _dot((a * a).astype(jnp.bfloat16), w2_ref[f * MLP_TF:(f + 1) * MLP_TF, :])
        acc = part if acc is None else acc + part
    return h1 + gate_ref[0] * acc


def _resident(shape, index=None):
    if index is None:
        return pl.BlockSpec(shape, lambda i: (0,) * len(shape), pipeline_mode=pl.Buffered(1))
    return pl.BlockSpec((None,) + tuple(shape), lambda i: (index,) + (0,) * len(shape),
                        pipeline_mode=pl.Buffered(1))


def _mlp_specs(layer, ms):
    return [pl.BlockSpec((1, 1, D), lambda i: (layer, 0, 0)), ms(3), ms(4), ms(5),
            _resident((D, D_FF), layer), _resident((D_FF, D), layer)]


def _sgu_in_kernel(x_ref, g_ref, sh_ref, sc_ref, w_ref, b_ref, lg_ref, lb_ref, o_ref):
    hn = _norm_mod(x_ref[...], g_ref[0], sh_ref[0], sc_ref[0]).astype(jnp.bfloat16)
    tn = SG_IN_CHUNK

    def act(c):
        return _gelu_tanh(_dot(hn, w_ref[:, c * tn:(c + 1) * tn]) + b_ref[:, c * tn:(c + 1) * tn])

    vs = [act(c) for c in range(SG_HALF // tn, 2 * SG_HALF // tn)]
    mu = sum(jnp.sum(v, axis=-1, keepdims=True) for v in vs) * (1.0 / SG_HALF)
    vs = [v - mu for v in vs]
    var = sum(jnp.sum(v * v, axis=-1, keepdims=True) for v in vs) * (1.0 / SG_HALF)
    rstd = jax.lax.rsqrt(var + EPS)
    for c, v in enumerate(vs):
        cols = slice(c * tn, (c + 1) * tn)
        o_ref[:, SG_HALF + c * tn:SG_HALF + (c + 1) * tn] = (
            v * rstd * lg_ref[:, cols] + lb_ref[:, cols]).astype(jnp.bfloat16)
    for c in range(SG_HALF // tn):
        o_ref[:, c * tn:(c + 1) * tn] = act(c).astype(jnp.bfloat16)


def _sgu_in(h, mods, norm_g, w_in, b_in, ln_g, ln_b, layer, rows_per_batch, tm):
    m = h.shape[0]
    ms = functools.partial(_mod_spec, layer, tm=tm, rows_per_batch=rows_per_batch)
    return pl.pallas_call(
        _sgu_in_kernel,
        out_shape=jax.ShapeDtypeStruct((m, 2 * SG_HALF), jnp.bfloat16),
        grid=(m // tm,),
        in_specs=[
            pl.BlockSpec((tm, D), lambda i: (i, 0)),
            pl.BlockSpec((1, 1, D), lambda i: (layer, 0, 0)),
            ms(0), ms(1),
            _resident((D, 2 * SG_HALF), layer // N_MIXERS),
            _resident((1, 2 * SG_HALF)),
            _resident((1, SG_HALF)),
            _resident((1, SG_HALF)),
        ],
        out_specs=pl.BlockSpec((tm, 2 * SG_HALF), lambda i: (i, 0)),
        compiler_params=_cparams(("parallel",)),
        name="sgu_in",
    )(h, norm_g, mods, mods, w_in, b_in, ln_g, ln_b)


def _sgu_out_kernel(x_ref, gate_ref, u_ref, v_ref, ws_ref, bs_ref, wo_ref, *rest):
    *mlp_refs, o_ref, gated_ref = rest
    tm = x_ref.shape[0]
    for ci in range(tm // CHUNK):
        rows = slice(ci * CHUNK, (ci + 1) * CHUNK)
        for g in range(SG_GROUPS):
            cols = slice(g * SG_GW, (g + 1) * SG_GW)
            vs = _dot(ws_ref[g], v_ref[rows, cols]) + bs_ref[:, cols]
            gated_ref[rows, cols] = (u_ref[rows, cols].astype(jnp.float32) * vs).astype(jnp.bfloat16)
    h1 = x_ref[...] + gate_ref[0] * _dot(gated_ref[...], wo_ref[...])
    o_ref[...] = _mlp_tail(h1, *mlp_refs)


def _sgu_out(h, mods, uv, w_s, bs_full, w_out, norm_g, w1, w2, layer, rows_per_batch, tm):
    m = h.shape[0]
    ms = functools.partial(_mod_spec, layer, tm=tm, rows_per_batch=rows_per_batch)
    return pl.pallas_call(
        _sgu_out_kernel,
        out_shape=jax.ShapeDtypeStruct((m, D), jnp.float32),
        grid=(m // tm,),
        in_specs=[
            pl.BlockSpec((tm, D), lambda i: (i, 0)),
            ms(2),
            pl.BlockSpec((tm, SG_HALF), lambda i: (i, 0)),
            pl.BlockSpec((tm, SG_HALF), lambda i: (i, 1)),
            _resident((SG_GROUPS, CHUNK, CHUNK), layer // N_MIXERS),
            _resident((CHUNK, SG_HALF)),
            _resident((SG_HALF, D), layer // N_MIXERS),
        ] + _mlp_specs(layer, ms),
        out_specs=pl.BlockSpec((tm, D), lambda i: (i, 0)),
        scratch_shapes=[pltpu.VMEM((tm, SG_HALF), jnp.bfloat16)],
        compiler_params=_cparams(("parallel",)),
        name="sgu_out_mlp",
    )(h, mods, uv, uv, w_s, bs_full, w_out, norm_g, mods, mods, mods, w1, w2)


def _qkv_kernel(x_ref, g_ref, sh_ref, sc_ref, w_ref, hg_ref, o_ref, *, n_first):
    hn = _norm_mod(x_ref[...], g_ref[0], sh_ref[0], sc_ref[0]).astype(jnp.bfloat16)
    lane = jax.lax.broadcasted_iota(jnp.int32, (1, 2 * NA_DH), 1)
    first = lane < NA_DH
    for n in range(n_first, 3):
        y = _dot(hn, w_ref[:, n * D:(n + 1) * D])
        out0 = (n - n_first) * D
        if n == 2:
            o_ref[:, out0:out0 + D] = y.astype(jnp.bfloat16)
            continue
        for j in range(D // (2 * NA_DH)):
            t = y[:, j * 2 * NA_DH:(j + 1) * 2 * NA_DH]
            t2 = t * t
            s_all = jnp.sum(t2, axis=-1, keepdims=True)
            s_first = jnp.sum(jnp.where(first, t2, 0.0), axis=-1, keepdims=True)
            ms = jnp.where(first, s_first, s_all - s_first) * (1.0 / NA_DH)
            gain = hg_ref[:, n * D + j * 2 * NA_DH:n * D + (j + 1) * 2 * NA_DH]
            o_ref[:, out0 + j * 2 * NA_DH:out0 + (j + 1) * 2 * NA_DH] = (
                t * jax.lax.rsqrt(ms + EPS) * gain).astype(jnp.bfloat16)


def _qkv(h, mods, norm_g, w, head_gains, layer, rows_per_batch, tm, n_first):
    m = h.shape[0]
    nd = (3 - n_first) * D
    ms = functools.partial(_mod_spec, layer, tm=tm, rows_per_batch=rows_per_batch)
    return pl.pallas_call(
        functools.partial(_qkv_kernel, n_first=n_first),
        out_shape=jax.ShapeDtypeStruct((m, nd), jnp.bfloat16),
        grid=(m // tm,),
        in_specs=[
            pl.BlockSpec((tm, D), lambda i: (i, 0)),
            pl.BlockSpec((1, 1, D), lambda i: (layer, 0, 0)),
            ms(0), ms(1),
            _resident((D, 3 * D), layer // N_MIXERS),
            _resident((1, 3 * D)),
        ],
        out_specs=pl.BlockSpec((tm, nd), lambda i: (i, 0)),
        compiler_params=_cparams(("parallel",)),
        name="na_qkv",
    )(h, norm_g, mods, mods, w, head_gains)


def _na_band_start(rb, rows):
    return np.clip(rb * NA_R - NA_KH // 2, 0, rows - NA_BAND)


NA_MASKED = 2 * NA_KH - 1


def _na_pair_codes():
    off = range(2 * NA_KH - 1)
    return ([(NA_MASKED, NA_MASKED)] + [(e, e + 1) for e in off[:-1]]
            + [(e, NA_MASKED) for e in off] + [(NA_MASKED, e) for e in off])


def _na_bias_index(rows):
    codes = {pair: n for n, pair in enumerate(_na_pair_codes())}
    idx = np.zeros((rows // NA_R, NA_R, NA_BAND // 2), np.int32)
    for rb in range(rows // NA_R):
        kb = _na_band_start(rb, rows)
        for i in range(NA_R):
            r = rb * NA_R + i
            rs = np.clip(r - NA_KH // 2, 0, rows - NA_KH)
            rel = [kb + j - r + (NA_KH - 1) if rs <= kb + j < rs + NA_KH else NA_MASKED for j in range(NA_BAND)]
            for jp in range(NA_BAND // 2):
                idx[rb, i, jp] = codes[(rel[2 * jp], rel[2 * jp + 1])]
    return idx.reshape(-1)


def _na_kernel(idx_ref, q_ref, k_ref, v_ref, kc_ref, vc_ref, bias_ref, o_ref, s0_ref, s1_ref, p0_ref, p1_ref):
    g = pl.program_id(2)
    last = pl.num_programs(2) - 1
    tq = NA_R * GRID_W
    s_refs, p_refs = (s0_ref, s1_ref), (p0_ref, p1_ref)

    def scores(j):
        s_refs[j][...] = _na_scores(2 * g + j, q_ref[0, j * tq:(j + 1) * tq, :], k_ref, kc_ref[0])

    def probs(j):
        p_refs[1 - j][...] = _na_probs(2 * g + j - 1, idx_ref, s_refs[1 - j][...], bias_ref)

    def values(j):
        o_ref[0, j * tq:(j + 1) * tq, :] = _na_values(2 * g + j - 2, p_refs[j][...], v_ref, vc_ref[0])

    @pl.when(g == 0)
    def _():
        scores(0)
        scores(1)
        probs(1)

    @pl.when(jnp.logical_and(g > 0, g < last))
    def _():
        for j in range(2):
            scores(j)
            probs(j)
            values(j)

    @pl.when(g == last)
    def _():
        probs(0)
        values(0)
        values(1)


def _na_band(rb, ref, tail):
    rows = ref.shape[1] // GRID_W
    kb = jnp.clip(rb * NA_R - NA_KH // 2, 0, rows - NA_BAND)
    start = pl.multiple_of(kb * GRID_W, GRID_W)
    return jnp.concatenate([ref[0, pl.ds(start, NA_BAND * GRID_W), :], tail], axis=0)


def _na_head_lanes():
    first = jax.lax.broadcasted_iota(jnp.int32, (1, 2 * NA_DH), 1) < NA_DH
    return first, jnp.logical_not(first)


def _na_scores(rb, q, k_ref, kc):
    qm = jnp.concatenate([jnp.where(keep, q, jnp.zeros_like(q)) for keep in _na_head_lanes()], axis=0)
    return jax.lax.dot_general(qm, _na_band(rb, k_ref, kc), (((1,), (1,)), ((), ())),
                               preferred_element_type=jnp.float32)


def _na_probs(rb, idx_ref, s, bias_ref):
    tq = s.shape[0] // 2
    n_loc = NA_BAND * GRID_W
    out = []
    for hh in range(2):
        bias_rows = []
        for i in range(NA_R):
            base = (rb * NA_R + i) * (NA_BAND // 2)
            pieces = [bias_ref[hh, idx_ref[base + jp]] for jp in range(NA_BAND // 2)]
            bias_rows.append(jnp.concatenate(pieces, axis=1))
        s_loc = s[hh * tq:(hh + 1) * tq, :n_loc] + jnp.concatenate(bias_rows, axis=0)
        s_ctx = s[hh * tq:(hh + 1) * tq, n_loc:]
        mx = jnp.maximum(jnp.max(s_loc, axis=-1, keepdims=True), jnp.max(s_ctx, axis=-1, keepdims=True))
        out.append(jnp.concatenate([jnp.exp(s_loc - mx), jnp.exp(s_ctx - mx)], axis=1).astype(jnp.bfloat16))
    return jnp.concatenate(out, axis=0)


def _na_values(rb, p, v_ref, vc):
    tq = p.shape[0] // 2
    first, second = _na_head_lanes()
    vcat = _na_band(rb, v_ref, vc)
    one = jnp.ones((), jnp.bfloat16)
    outs = []
    for hh, keep in enumerate((first, second)):
        o = _dot(p[hh * tq:(hh + 1) * tq], jnp.where(keep, vcat, one))
        outs.append(o / pltpu.roll(o, NA_DH, axis=1))
    return jnp.where(first, outs[0], outs[1]).astype(jnp.bfloat16)


def _na_bias_table(rpb):
    cols = np.arange(GRID_W)
    cs = np.clip(cols - NA_KW // 2, 0, GRID_W - NA_KW)
    rel = cols[None, :] - cols[:, None] + (NA_KW - 1)
    ok = (cols[None, :] >= cs[:, None]) & (cols[None, :] < cs[:, None] + NA_KW)
    onehot = (ok[None] & (rel[None] == np.arange(2 * NA_KW - 1)[:, None, None])).astype(np.float32)
    dense = jnp.einsum('het,tqk->heqk', rpb, onehot, precision=jax.lax.Precision.HIGHEST)
    dense = jnp.where(ok[None, None], dense, NEG)
    dense = jnp.concatenate([dense, jnp.full((NA_HEADS, 1, GRID_W, GRID_W), NEG, jnp.float32)], axis=1)
    return jnp.stack([jnp.concatenate([dense[:, left], dense[:, right]], axis=-1)
                      for left, right in _na_pair_codes()], axis=1)


def _na_attention(qkv, kvc, bias, bsz, seq, ctx_len):
    hp = NA_HEADS // 2
    bw = 2 * NA_DH
    tq = 2 * NA_R * GRID_W
    steps = seq // tq
    n_keys = NA_BAND * GRID_W + ctx_len
    grid_spec = pltpu.PrefetchScalarGridSpec(
        num_scalar_prefetch=1,
        grid=(hp, bsz, steps + 1),
        in_specs=[
            pl.BlockSpec((1, tq, bw), lambda h, b, r, idx: (b, jnp.minimum(r, steps - 1), h)),
            pl.BlockSpec((1, seq, bw), lambda h, b, r, idx: (b, 0, hp + h)),
            pl.BlockSpec((1, seq, bw), lambda h, b, r, idx: (b, 0, 2 * hp + h)),
            pl.BlockSpec((1, ctx_len, bw), lambda h, b, r, idx: (b, 0, h)),
            pl.BlockSpec((1, ctx_len, bw), lambda h, b, r, idx: (b, 0, hp + h)),
            pl.BlockSpec((2,) + bias.shape[1:], lambda h, b, r, idx: (h, 0, 0, 0)),
        ],
        out_specs=pl.BlockSpec((1, tq, bw), lambda h, b, r, idx: (b, jnp.maximum(r - 1, 0), h)),
        scratch_shapes=[pltpu.VMEM((2 * NA_R * GRID_W, n_keys), dt)
                        for dt in (jnp.float32, jnp.float32, jnp.bfloat16, jnp.bfloat16)],
    )
    return pl.pallas_call(
        _na_kernel,
        out_shape=jax.ShapeDtypeStruct((bsz, seq, D), jnp.bfloat16),
        grid_spec=grid_spec,
        compiler_params=_cparams(("parallel", "parallel", "arbitrary")),
        name="na_attention",
    )(jnp.asarray(_na_bias_index(seq // GRID_W)), qkv, qkv, qkv, kvc, kvc, bias)


def _proj_res_kernel(x_ref, gate_ref, a_ref, w_ref, *rest):
    *mlp_refs, o_ref = rest
    h1 = x_ref[...] + gate_ref[0] * _dot(a_ref[...], w_ref[...])
    o_ref[...] = _mlp_tail(h1, *mlp_refs)


def _proj_res(h, mods, a, w, norm_g, w1, w2, layer, rows_per_batch, tm):
    m = h.shape[0]
    ms = functools.partial(_mod_spec, layer, tm=tm, rows_per_batch=rows_per_batch)
    return pl.pallas_call(
        _proj_res_kernel,
        out_shape=jax.ShapeDtypeStruct((m, D), jnp.float32),
        grid=(m // tm,),
        in_specs=[
            pl.BlockSpec((tm, D), lambda i: (i, 0)),
            ms(2),
            pl.BlockSpec((tm, D), lambda i: (i, 0)),
            _resident((D, D), layer // N_MIXERS),
        ] + _mlp_specs(layer, ms),
        out_specs=pl.BlockSpec((tm, D), lambda i: (i, 0)),
        compiler_params=_cparams(("parallel",)),
        name="na_out_proj_mlp",
    )(h, mods, a, w, norm_g, mods, mods, mods, w1, w2)


def _glu_kernel(x_ref, g_ref, sh_ref, sc_ref, w_ref, b_ref, o_ref):
    hn = _norm_mod(x_ref[...], g_ref[0], sh_ref[0], sc_ref[0]).astype(jnp.bfloat16)
    y = _dot(hn, w_ref[...]) + b_ref[...]
    o_ref[...] = (y[:, :D] * jax.nn.sigmoid(y[:, D:])).astype(jnp.bfloat16)


def _glu(h, mods, norm_g, w, b, layer, rows_per_batch, tm):
    m = h.shape[0]
    ms = functools.partial(_mod_spec, layer, tm=tm, rows_per_batch=rows_per_batch)
    return pl.pallas_call(
        _glu_kernel,
        out_shape=jax.ShapeDtypeStruct((m, D), jnp.bfloat16),
        grid=(m // tm,),
        in_specs=[
            pl.BlockSpec((tm, D), lambda i: (i, 0)),
            pl.BlockSpec((1, 1, D), lambda i: (layer, 0, 0)),
            ms(0), ms(1),
            _resident((D, 2 * D), layer // N_MIXERS),
            _resident((1, 2 * D)),
        ],
        out_specs=pl.BlockSpec((tm, D), lambda i: (i, 0)),
        compiler_params=_cparams(("parallel",)),
        name="conv_glu",
    )(h, norm_g, mods, mods, w, b)


def _conv_kernel(x_ref, gate_ref, a_ref, prev_ref, next_ref, wd_ref, bd_ref, lg_ref, lb_ref,
                 w2_ref, b2_ref, *rest, tiles_per_seq):
    *mlp_refs, o_ref, win_ref, conv_ref = rest
    tm = x_ref.shape[0]
    i = pl.program_id(0)
    pos = i % tiles_per_seq
    prev = jnp.where(pos == 0, 0.0, prev_ref[...].astype(jnp.float32))
    nxt = jnp.where(pos == tiles_per_seq - 1, 0.0, next_ref[...].astype(jnp.float32))
    a = a_ref[...].astype(jnp.float32)
    n_slab = D // LANES
    for s in range(n_slab):
        cols = slice(s * LANES, (s + 1) * LANES)
        win_ref[s, 0:HALO, :] = prev[:, cols]
        win_ref[s, HALO:HALO + tm, :] = a[:, cols]
        win_ref[s, HALO + tm:, :] = nxt[:, cols]
    blk = PHASES * SUBLANES
    for s in range(n_slab):
        cols = slice(s * LANES, (s + 1) * LANES)
        taps = [jnp.broadcast_to(wd_ref[k:k + 1, cols], (SUBLANES, LANES)) for k in range(CONV_W)]

        def block(rb, carry, s=s, taps=taps):
            base = pl.multiple_of(rb * (CONV_UNROLL * blk), CONV_UNROLL * blk)
            for q in range(CONV_UNROLL * PHASES):
                row = (q // PHASES) * blk + q % PHASES
                acc = None
                for k in range(CONV_W):
                    off = row + k + HALO - CONV_PAD
                    term = win_ref[s, pl.ds(base + off, SUBLANES, stride=PHASES), :] * taps[k]
                    acc = term if acc is None else acc + term
                conv_ref[s, pl.ds(base + row, SUBLANES, stride=PHASES), :] = acc
            return carry

        jax.lax.fori_loop(0, tm // (CONV_UNROLL * blk), block, 0)
    conv = jnp.concatenate([conv_ref[s] for s in range(n_slab)], axis=1)
    y = _layer_norm(conv + bd_ref[...], lg_ref[...], lb_ref[...])
    y = (y * jax.nn.sigmoid(y)).astype(jnp.bfloat16)
    h1 = x_ref[...] + gate_ref[0] * (_dot(y, w2_ref[...]) + b2_ref[...])
    o_ref[...] = _mlp_tail(h1, *mlp_refs)


def _conv_mix(h, mods, a, w_dw, b_dw, ln_g, ln_b, w_pw2, b_pw2, norm_g, w1, w2, layer, rows_per_batch, tm):
    m = h.shape[0]
    ms = functools.partial(_mod_spec, layer, tm=tm, rows_per_batch=rows_per_batch)
    hb = tm // HALO
    last = m // HALO - 1
    vec = pl.BlockSpec((1, D), lambda i: (0, 0))
    return pl.pallas_call(
        functools.partial(_conv_kernel, tiles_per_seq=rows_per_batch // tm),
        out_shape=jax.ShapeDtypeStruct((m, D), jnp.float32),
        grid=(m // tm,),
        in_specs=[
            pl.BlockSpec((tm, D), lambda i: (i, 0)),
            ms(2),
            pl.BlockSpec((tm, D), lambda i: (i, 0)),
            pl.BlockSpec((HALO, D), lambda i: (jnp.maximum(i * hb - 1, 0), 0)),
            pl.BlockSpec((HALO, D), lambda i: (jnp.minimum((i + 1) * hb, last), 0)),
            pl.BlockSpec((CONV_W, D), lambda i: (0, 0)),
            vec, vec, vec,
            _resident((D, D), layer // N_MIXERS),
            vec,
        ] + _mlp_specs(layer, ms),
        out_specs=pl.BlockSpec((tm, D), lambda i: (i, 0)),
        scratch_shapes=[pltpu.VMEM((D // LANES, tm + 2 * HALO, LANES), jnp.float32),
                        pltpu.VMEM((D // LANES, tm, LANES), jnp.float32)],
        compiler_params=_cparams(("parallel",)),
        name="conv_mix_mlp",
    )(h, mods, a, a, a, w_dw, b_dw, ln_g, ln_b, w_pw2, b_pw2, norm_g, mods, mods, mods, w1, w2)


def kernel(x, c, ctx, c_ctx, ada_w, ada_b, norm1_g, norm2_g, a_w_in, a_b_in, a_ln_g, a_ln_b, a_w_s, a_b_s, a_w_out, b_w_qkv, b_w_o, b_q_g, b_k_g, b_rpb, c_w_pw1, c_b_pw1, c_w_dw, c_b_dw, c_ln_g, c_ln_b, c_w_pw2, c_b_pw2, mlp_w1, mlp_w2):
    bsz, seq, _ = x.shape
    ctx_len = ctx.shape[1]
    bf = jnp.bfloat16
    assert bsz <= CTX_ROW and seq % (2 * NA_R * GRID_W) == 0 and seq // GRID_W >= NA_BAND

    cond = jnp.zeros((MOD_ROWS, D), jnp.float32).at[:bsz].set(c).at[CTX_ROW].set(c_ctx)
    mods = _ada_mods(cond, ada_w, ada_b)
    n1 = norm1_g.reshape(DEPTH, 1, D)
    n2 = norm2_g.reshape(DEPTH, 1, D)

    h = x.reshape(bsz * seq, D)
    hc = ctx.reshape(bsz * ctx_len, D)
    tm, tmc = 512, 256
    w1, w2 = mlp_w1.astype(bf), mlp_w2.astype(bf)
    w_in, w_out, w_s = a_w_in.astype(bf), a_w_out.astype(bf), a_w_s.astype(bf)
    w_qkv, w_o = b_w_qkv.astype(bf), b_w_o.astype(bf)
    w_pw1, w_pw2 = c_w_pw1.astype(bf), c_w_pw2.astype(bf)

    for l in range(DEPTH):
        kind, j = l % N_MIXERS, l // N_MIXERS
        if kind == 0:
            b_in = a_b_in[j].reshape(1, -1)
            ln_g, ln_b = a_ln_g[j].reshape(1, -1), a_ln_b[j].reshape(1, -1)
            bs_full = jnp.repeat(a_b_s[j].T, SG_GW, axis=1)
            uv = _sgu_in(h, mods, n1, w_in, b_in, ln_g, ln_b, l, seq, tm)
            h = _sgu_out(h, mods, uv, w_s, bs_full, w_out, n2, w1, w2, l, seq, tm)
            if l < LAST_CTX_READER:
                uvc = _sgu_in(hc, mods, n1, w_in, b_in, ln_g, ln_b, l, None, tmc)
                hc = _sgu_out(hc, mods, uvc, w_s, bs_full, w_out, n2, w1, w2, l, None, tmc)
        elif kind == 1:
            gains = jnp.concatenate([jnp.tile(b_q_g[j], NA_HEADS) * NA_DH ** -0.5,
                                     jnp.tile(b_k_g[j], NA_HEADS),
                                     jnp.ones((D,), jnp.float32)]).reshape(1, 3 * D)
            qkv = _qkv(h, mods, n1, w_qkv, gains, l, seq, tm, 0)
            kvc = _qkv(hc, mods, n1, w_qkv, gains, l, None, tmc, 1)
            o = _na_attention(qkv.reshape(bsz, seq, 3 * D), kvc.reshape(bsz, ctx_len, 2 * D),
                              _na_bias_table(b_rpb[j]), bsz, seq, ctx_len)
            h = _proj_res(h, mods, o.reshape(bsz * seq, D), w_o, n2, w1, w2, l, seq, tm)
        else:
            a = _glu(h, mods, n1, w_pw1, c_b_pw1[j].reshape(1, -1), l, seq, tm)
            h = _conv_mix(h, mods, a, c_w_dw[j], c_b_dw[j].reshape(1, -1), c_ln_g[j].reshape(1, -1),
                          c_ln_b[j].reshape(1, -1), w_pw2, c_b_pw2[j].reshape(1, -1),
                          n2, w1, w2, l, seq, 256)
    return h.reshape(bsz, seq, D)
```

```python
import functools

import jax
import jax.numpy as jnp
import numpy as np
from jax.experimental import pallas as pl
from jax.experimental.pallas import tpu as pltpu

D = 1024
DEPTH = 4
GRID_W = 64
N_MIXERS = 3
CHUNK = 128
SG_GROUPS = 8
SG_HALF = 2 * D
SG_GW = SG_HALF // SG_GROUPS
SG_IN_CHUNK = 1024
NA_HEADS = 16
NA_DH = D // NA_HEADS
NA_KH = 8
NA_KW = 16
CONV_W = 31
CONV_PAD = CONV_W // 2
D_FF = 4 * D
MLP_TF = 1024
EPS = 1e-6
LAST_CTX_READER = max(range(1, DEPTH, N_MIXERS))

MOD_ROWS = 16
CTX_ROW = 8
NEG = -1e30
LOG2E = float(np.log2(np.e))
VMEM_LIMIT = 56 * 1024 * 1024

NA_R = 4
NA_BAND = 12
HALO = 16
SUBLANES, LANES = 8, 128
PHASES = 4
CONV_UNROLL = 4


def _cparams(sem):
    return pltpu.CompilerParams(dimension_semantics=sem, vmem_limit_bytes=VMEM_LIMIT)


def _norm_mod(x, g, sh, sc):
    ms = jnp.mean(x * x, axis=-1, keepdims=True)
    return x * jax.lax.rsqrt(ms + EPS) * (g * (1.0 + sc)) + sh


def _gelu_tanh(x):
    c = -2.0 * np.sqrt(2.0 / np.pi) * np.log2(np.e)
    return x / (1.0 + jnp.exp2(x * (c + (0.044715 * c) * (x * x))))


def _layer_norm(x, g, b):
    mu = jnp.mean(x, axis=-1, keepdims=True)
    xc = x - mu
    var = jnp.mean(xc * xc, axis=-1, keepdims=True)
    return xc * jax.lax.rsqrt(var + EPS) * g + b


def _dot(a, b):
    return jnp.dot(a, b, preferred_element_type=jnp.float32)


def _ada_kernel(c_ref, w_ref, b_ref, o_ref):
    cond = c_ref[...]
    act = (cond * jax.nn.sigmoid(cond)).astype(jnp.bfloat16)
    o_ref[0] = _dot(act, w_ref[0].astype(jnp.bfloat16)) + b_ref[0]


def _ada_mods(cond, ada_w, ada_b):
    tn = 1536
    out = pl.pallas_call(
        _ada_kernel,
        out_shape=jax.ShapeDtypeStruct((DEPTH, MOD_ROWS, 6 * D), jnp.float32),
        grid=(DEPTH, 6 * D // tn),
        in_specs=[
            pl.BlockSpec((MOD_ROWS, D), lambda l, n: (0, 0)),
            pl.BlockSpec((1, D, tn), lambda l, n: (l, 0, n)),
            pl.BlockSpec((1, 1, tn), lambda l, n: (l, 0, n)),
        ],
        out_specs=pl.BlockSpec((1, MOD_ROWS, tn), lambda l, n: (l, 0, n)),
        compiler_params=_cparams(("arbitrary", "arbitrary")),
        name="ada_mods",
    )(cond, ada_w, ada_b.reshape(DEPTH, 1, 6 * D))
    return out.reshape(DEPTH * MOD_ROWS, 1, 6 * D)


def _mod_spec(layer, chunk, tm, rows_per_batch):
    if rows_per_batch is None:
        return pl.BlockSpec((1, 1, D), lambda i, *_: (layer * MOD_ROWS + CTX_ROW, 0, chunk))
    return pl.BlockSpec(
        (1, 1, D), lambda i, *_: (layer * MOD_ROWS + (i * tm) // rows_per_batch, 0, chunk))


def _mlp_tail(h1, g_ref, sh_ref, sc_ref, gate_ref, w1_ref, w2_ref):
    hn = _norm_mod(h1, g_ref[0], sh_ref[0], sc_ref[0]).astype(jnp.bfloat16)
    acc = None
    for f in range(D_FF // MLP_TF):
        a = jnp.maximum(_dot(hn, w1_ref[:, f * MLP_TF:(f + 1) * MLP_TF]), 0.0)
        part = _dot((a * a).astype(jnp.bfloat16), w2_ref[f * MLP_TF:(f + 1) * MLP_TF, :])
        acc = part if acc is None else acc + part
    return h1 + gate_ref[0] * acc


def _resident(shape, index=None):
    if index is None:
        return pl.BlockSpec(shape, lambda i: (0,) * len(shape), pipeline_mode=pl.Buffered(1))
    return pl.BlockSpec((None,) + tuple(shape), lambda i: (index,) + (0,) * len(shape),
                        pipeline_mode=pl.Buffered(1))


def _mlp_specs(layer, ms):
    return [pl.BlockSpec((1, 1, D), lambda i: (layer, 0, 0)), ms(3), ms(4), ms(5),
            _resident((D, D_FF), layer), _resident((D_FF, D), layer)]


def _sgu_in_kernel(x_ref, g_ref, sh_ref, sc_ref, w_ref, b_ref, lg_ref, lb_ref, o_ref):
    hn = _norm_mod(x_ref[...], g_ref[0], sh_ref[0], sc_ref[0]).astype(jnp.bfloat16)
    tn = SG_IN_CHUNK

    def act(c):
        return _gelu_tanh(_dot(hn, w_ref[:, c * tn:(c + 1) * tn]) + b_ref[:, c * tn:(c + 1) * tn])

    vs = [act(c) for c in range(SG_HALF // tn, 2 * SG_HALF // tn)]
    mu = sum(jnp.sum(v, axis=-1, keepdims=True) for v in vs) * (1.0 / SG_HALF)
    vs = [v - mu for v in vs]
    var = sum(jnp.sum(v * v, axis=-1, keepdims=True) for v in vs) * (1.0 / SG_HALF)
    rstd = jax.lax.rsqrt(var + EPS)
    for c, v in enumerate(vs):
        cols = slice(c * tn, (c + 1) * tn)
        o_ref[:, SG_HALF + c * tn:SG_HALF + (c + 1) * tn] = (
            v * rstd * lg_ref[:, cols] + lb_ref[:, cols]).astype(jnp.bfloat16)
    for c in range(SG_HALF // tn):
        o_ref[:, c * tn:(c + 1) * tn] = act(c).astype(jnp.bfloat16)


def _sgu_in(h, mods, norm_g, w_in, b_in, ln_g, ln_b, layer, rows_per_batch, tm):
    m = h.shape[0]
    ms = functools.partial(_mod_spec, layer, tm=tm, rows_per_batch=rows_per_batch)
    return pl.pallas_call(
        _sgu_in_kernel,
        out_shape=jax.ShapeDtypeStruct((m, 2 * SG_HALF), jnp.bfloat16),
        grid=(m // tm,),
        in_specs=[
            pl.BlockSpec((tm, D), lambda i: (i, 0)),
            pl.BlockSpec((1, 1, D), lambda i: (layer, 0, 0)),
            ms(0), ms(1),
            _resident((D, 2 * SG_HALF), layer // N_MIXERS),
            _resident((1, 2 * SG_HALF)),
            _resident((1, SG_HALF)),
            _resident((1, SG_HALF)),
        ],
        out_specs=pl.BlockSpec((tm, 2 * SG_HALF), lambda i: (i, 0)),
        compiler_params=_cparams(("parallel",)),
        name="sgu_in",
    )(h, norm_g, mods, mods, w_in, b_in, ln_g, ln_b)


def _sgu_out_kernel(x_ref, gate_ref, u_ref, v_ref, ws_ref, bs_ref, wo_ref, *rest):
    *mlp_refs, o_ref, gated_ref = rest
    tm = x_ref.shape[0]
    for ci in range(tm // CHUNK):
        rows = slice(ci * CHUNK, (ci + 1) * CHUNK)
        for g in range(SG_GROUPS):
            cols = slice(g * SG_GW, (g + 1) * SG_GW)
            vs = _dot(ws_ref[g], v_ref[rows, cols]) + bs_ref[:, cols]
            gated_ref[rows, cols] = (u_ref[rows, cols].astype(jnp.float32) * vs).astype(jnp.bfloat16)
    h1 = x_ref[...] + gate_ref[0] * _dot(gated_ref[...], wo_ref[...])
    o_ref[...] = _mlp_tail(h1, *mlp_refs)


def _sgu_out(h, mods, uv, w_s, bs_full, w_out, norm_g, w1, w2, layer, rows_per_batch, tm):
    m = h.shape[0]
    ms = functools.partial(_mod_spec, layer, tm=tm, rows_per_batch=rows_per_batch)
    return pl.pallas_call(
        _sgu_out_kernel,
        out_shape=jax.ShapeDtypeStruct((m, D), jnp.float32),
        grid=(m // tm,),
        in_specs=[
            pl.BlockSpec((tm, D), lambda i: (i, 0)),
            ms(2),
            pl.BlockSpec((tm, SG_HALF), lambda i: (i, 0)),
            pl.BlockSpec((tm, SG_HALF), lambda i: (i, 1)),
            _resident((SG_GROUPS, CHUNK, CHUNK), layer // N_MIXERS),
            _resident((CHUNK, SG_HALF)),
            _resident((SG_HALF, D), layer // N_MIXERS),
        ] + _mlp_specs(layer, ms),
        out_specs=pl.BlockSpec((tm, D), lambda i: (i, 0)),
        scratch_shapes=[pltpu.VMEM((tm, SG_HALF), jnp.bfloat16)],
        compiler_params=_cparams(("parallel",)),
        name="sgu_out_mlp",
    )(h, mods, uv, uv, w_s, bs_full, w_out, norm_g, mods, mods, mods, w1, w2)


def _qkv_kernel(x_ref, g_ref, sh_ref, sc_ref, w_ref, hg_ref, o_ref, *, n_first):
    hn = _norm_mod(x_ref[...], g_ref[0], sh_ref[0], sc_ref[0]).astype(jnp.bfloat16)
    lane = jax.lax.broadcasted_iota(jnp.int32, (1, 2 * NA_DH), 1)
    first = lane < NA_DH
    for n in range(n_first, 3):
        y = _dot(hn, w_ref[:, n * D:(n + 1) * D])
        out0 = (n - n_first) * D
        if n == 2:
            o_ref[:, out0:out0 + D] = y.astype(jnp.bfloat16)
            continue
        for j in range(D // (2 * NA_DH)):
            t = y[:, j * 2 * NA_DH:(j + 1) * 2 * NA_DH]
            t2 = t * t
            s_all = jnp.sum(t2, axis=-1, keepdims=True)
            s_first = jnp.sum(jnp.where(first, t2, 0.0), axis=-1, keepdims=True)
            ms = jnp.where(first, s_first, s_all - s_first) * (1.0 / NA_DH)
            gain = hg_ref[:, n * D + j * 2 * NA_DH:n * D + (j + 1) * 2 * NA_DH]
            o_ref[:, out0 + j * 2 * NA_DH:out0 + (j + 1) * 2 * NA_DH] = (
                t * jax.lax.rsqrt(ms + EPS) * gain).astype(jnp.bfloat16)


def _qkv(h, mods, norm_g, w, head_gains, layer, rows_per_batch, tm, n_first):
    m = h.shape[0]
    nd = (3 - n_first) * D
    ms = functools.partial(_mod_spec, layer, tm=tm, rows_per_batch=rows_per_batch)
    return pl.pallas_call(
        functools.partial(_qkv_kernel, n_first=n_first),
        out_shape=jax.ShapeDtypeStruct((m, nd), jnp.bfloat16),
        grid=(m // tm,),
        in_specs=[
            pl.BlockSpec((tm, D), lambda i: (i, 0)),
            pl.BlockSpec((1, 1, D), lambda i: (layer, 0, 0)),
            ms(0), ms(1),
            _resident((D, 3 * D), layer // N_MIXERS),
            _resident((1, 3 * D)),
        ],
        out_specs=pl.BlockSpec((tm, nd), lambda i: (i, 0)),
        compiler_params=_cparams(("parallel",)),
        name="na_qkv",
    )(h, norm_g, mods, mods, w, head_gains)


def _na_band_start(rb, rows):
    return np.clip(rb * NA_R - NA_KH // 2, 0, rows - NA_BAND)


NA_MASKED = 2 * NA_KH - 1


def _na_pair_codes():
    off = range(2 * NA_KH - 1)
    return ([(NA_MASKED, NA_MASKED)] + [(e, e + 1) for e in off[:-1]]
            + [(e, NA_MASKED) for e in off] + [(NA_MASKED, e) for e in off])


def _na_bias_index(rows):
    codes = {pair: n for n, pair in enumerate(_na_pair_codes())}
    idx = np.zeros((rows // NA_R, NA_R, NA_BAND // 2), np.int32)
    for rb in range(rows // NA_R):
        kb = _na_band_start(rb, rows)
        for i in range(NA_R):
            r = rb * NA_R + i
            rs = np.clip(r - NA_KH // 2, 0, rows - NA_KH)
            rel = [kb + j - r + (NA_KH - 1) if rs <= kb + j < rs + NA_KH else NA_MASKED for j in range(NA_BAND)]
            for jp in range(NA_BAND // 2):
                idx[rb, i, jp] = codes[(rel[2 * jp], rel[2 * jp + 1])]
    return idx.reshape(-1)


def _na_kernel(idx_ref, q_ref, k_ref, v_ref, kc_ref, vc_ref, bias_ref, o_ref, s0_ref, s1_ref, p0_ref, p1_ref):
    g = pl.program_id(2)
    last = pl.num_programs(2) - 1
    tq = NA_R * GRID_W
    s_refs, p_refs = (s0_ref, s1_ref), (p0_ref, p1_ref)

    def scores(j):
        s_refs[j][...] = _na_scores(2 * g + j, q_ref[0, j * tq:(j + 1) * tq, :], k_ref, kc_ref[0])

    def probs(j):
        p_refs[1 - j][...] = _na_probs(2 * g + j - 1, idx_ref, s_refs[1 - j][...], bias_ref)

    def values(j):
        o_ref[0, j * tq:(j + 1) * tq, :] = _na_values(2 * g + j - 2, p_refs[j][...], v_ref, vc_ref[0])

    @pl.when(g == 0)
    def _():
        scores(0)
        scores(1)
        probs(1)

    @pl.when(jnp.logical_and(g > 0, g < last))
    def _():
        for j in range(2):
            scores(j)
            probs(j)
            values(j)

    @pl.when(g == last)
    def _():
        probs(0)
        values(0)
        values(1)


def _na_band(rb, ref, tail):
    rows = ref.shape[1] // GRID_W
    kb = jnp.clip(rb * NA_R - NA_KH // 2, 0, rows - NA_BAND)
    start = pl.multiple_of(kb * GRID_W, GRID_W)
    return jnp.concatenate([ref[0, pl.ds(start, NA_BAND * GRID_W), :], tail], axis=0)


def _na_head_lanes():
    first = jax.lax.broadcasted_iota(jnp.int32, (1, 2 * NA_DH), 1) < NA_DH
    return first, jnp.logical_not(first)


def _na_scores(rb, q, k_ref, kc):
    qm = jnp.concatenate([jnp.where(keep, q, jnp.zeros_like(q)) for keep in _na_head_lanes()], axis=0)
    return jax.lax.dot_general(qm, _na_band(rb, k_ref, kc), (((1,), (1,)), ((), ())),
                               preferred_element_type=jnp.float32)


def _na_probs(rb, idx_ref, s, bias_ref):
    tq = s.shape[0] // 2
    n_loc = NA_BAND * GRID_W
    out = []
    for hh in range(2):
        bias_rows = []
        for i in range(NA_R):
            base = (rb * NA_R + i) * (NA_BAND // 2)
            pieces = [bias_ref[hh, idx_ref[base + jp]] for jp in range(NA_BAND // 2)]
            bias_rows.append(jnp.concatenate(pieces, axis=1))
        s_loc = s[hh * tq:(hh + 1) * tq, :n_loc] + jnp.concatenate(bias_rows, axis=0)
        s_ctx = s[hh * tq:(hh + 1) * tq, n_loc:]
        mx = jnp.maximum(jnp.max(s_loc, axis=-1, keepdims=True), jnp.max(s_ctx, axis=-1, keepdims=True))
        out.append(jnp.concatenate([jnp.exp2(s_loc - mx), jnp.exp2(s_ctx - mx)], axis=1).astype(jnp.bfloat16))
    return jnp.concatenate(out, axis=0)


def _na_values(rb, p, v_ref, vc):
    tq = p.shape[0] // 2
    first, second = _na_head_lanes()
    vcat = _na_band(rb, v_ref, vc)
    one = jnp.ones((), jnp.bfloat16)
    outs = []
    for hh, keep in enumerate((first, second)):
        o = _dot(p[hh * tq:(hh + 1) * tq], jnp.where(keep, vcat, one))
        outs.append(o / pltpu.roll(o, NA_DH, axis=1))
    return jnp.where(first, outs[0], outs[1]).astype(jnp.bfloat16)


def _na_bias_table(rpb):
    cols = np.arange(GRID_W)
    cs = np.clip(cols - NA_KW // 2, 0, GRID_W - NA_KW)
    rel = cols[None, :] - cols[:, None] + (NA_KW - 1)
    ok = (cols[None, :] >= cs[:, None]) & (cols[None, :] < cs[:, None] + NA_KW)
    onehot = (ok[None] & (rel[None] == np.arange(2 * NA_KW - 1)[:, None, None])).astype(np.float32)
    dense = jnp.einsum('het,tqk->heqk', rpb, onehot, precision=jax.lax.Precision.HIGHEST)
    dense = jnp.where(ok[None, None], dense * LOG2E, NEG)
    dense = jnp.concatenate([dense, jnp.full((NA_HEADS, 1, GRID_W, GRID_W), NEG, jnp.float32)], axis=1)
    return jnp.stack([jnp.concatenate([dense[:, left], dense[:, right]], axis=-1)
                      for left, right in _na_pair_codes()], axis=1)


def _na_attention(qkv, kvc, bias, bsz, seq, ctx_len):
    hp = NA_HEADS // 2
    bw = 2 * NA_DH
    tq = 2 * NA_R * GRID_W
    steps = seq // tq
    n_keys = NA_BAND * GRID_W + ctx_len
    grid_spec = pltpu.PrefetchScalarGridSpec(
        num_scalar_prefetch=1,
        grid=(hp, bsz, steps + 1),
        in_specs=[
            pl.BlockSpec((1, tq, bw), lambda h, b, r, idx: (b, jnp.minimum(r, steps - 1), h)),
            pl.BlockSpec((1, seq, bw), lambda h, b, r, idx: (b, 0, hp + h)),
            pl.BlockSpec((1, seq, bw), lambda h, b, r, idx: (b, 0, 2 * hp + h)),
            pl.BlockSpec((1, ctx_len, bw), lambda h, b, r, idx: (b, 0, h)),
            pl.BlockSpec((1, ctx_len, bw), lambda h, b, r, idx: (b, 0, hp + h)),
            pl.BlockSpec((2,) + bias.shape[1:], lambda h, b, r, idx: (h, 0, 0, 0)),
        ],
        out_specs=pl.BlockSpec((1, tq, bw), lambda h, b, r, idx: (b, jnp.maximum(r - 1, 0), h)),
        scratch_shapes=[pltpu.VMEM((2 * NA_R * GRID_W, n_keys), dt)
                        for dt in (jnp.float32, jnp.float32, jnp.bfloat16, jnp.bfloat16)],
    )
    return pl.pallas_call(
        _na_kernel,
        out_shape=jax.ShapeDtypeStruct((bsz, seq, D), jnp.bfloat16),
        grid_spec=grid_spec,
        compiler_params=_cparams(("parallel", "parallel", "arbitrary")),
        name="na_attention",
    )(jnp.asarray(_na_bias_index(seq // GRID_W)), qkv, qkv, qkv, kvc, kvc, bias)


def _proj_res_kernel(x_ref, gate_ref, a_ref, w_ref, *rest):
    *mlp_refs, o_ref = rest
    h1 = x_ref[...] + gate_ref[0] * _dot(a_ref[...], w_ref[...])
    o_ref[...] = _mlp_tail(h1, *mlp_refs)


def _proj_res(h, mods, a, w, norm_g, w1, w2, layer, rows_per_batch, tm):
    m = h.shape[0]
    ms = functools.partial(_mod_spec, layer, tm=tm, rows_per_batch=rows_per_batch)
    return pl.pallas_call(
        _proj_res_kernel,
        out_shape=jax.ShapeDtypeStruct((m, D), jnp.float32),
        grid=(m // tm,),
        in_specs=[
            pl.BlockSpec((tm, D), lambda i: (i, 0)),
            ms(2),
            pl.BlockSpec((tm, D), lambda i: (i, 0)),
            _resident((D, D), layer // N_MIXERS),
        ] + _mlp_specs(layer, ms),
        out_specs=pl.BlockSpec((tm, D), lambda i: (i, 0)),
        compiler_params=_cparams(("parallel",)),
        name="na_out_proj_mlp",
    )(h, mods, a, w, norm_g, mods, mods, mods, w1, w2)


def _glu_kernel(x_ref, g_ref, sh_ref, sc_ref, w_ref, b_ref, o_ref):
    hn = _norm_mod(x_ref[...], g_ref[0], sh_ref[0], sc_ref[0]).astype(jnp.bfloat16)
    y = _dot(hn, w_ref[...]) + b_ref[...]
    o_ref[...] = (y[:, :D] * jax.nn.sigmoid(y[:, D:])).astype(jnp.bfloat16)


def _glu(h, mods, norm_g, w, b, layer, rows_per_batch, tm):
    m = h.shape[0]
    ms = functools.partial(_mod_spec, layer, tm=tm, rows_per_batch=rows_per_batch)
    return pl.pallas_call(
        _glu_kernel,
        out_shape=jax.ShapeDtypeStruct((m, D), jnp.bfloat16),
        grid=(m // tm,),
        in_specs=[
            pl.BlockSpec((tm, D), lambda i: (i, 0)),
            pl.BlockSpec((1, 1, D), lambda i: (layer, 0, 0)),
            ms(0), ms(1),
            _resident((D, 2 * D), layer // N_MIXERS),
            _resident((1, 2 * D)),
        ],
        out_specs=pl.BlockSpec((tm, D), lambda i: (i, 0)),
        compiler_params=_cparams(("parallel",)),
        name="conv_glu",
    )(h, norm_g, mods, mods, w, b)


def _conv_kernel(x_ref, gate_ref, a_ref, prev_ref, next_ref, wd_ref, bd_ref, lg_ref, lb_ref,
                 w2_ref, b2_ref, *rest, tiles_per_seq):
    *mlp_refs, o_ref, win_ref, conv_ref = rest
    tm = x_ref.shape[0]
    i = pl.program_id(0)
    pos = i % tiles_per_seq
    prev = jnp.where(pos == 0, 0.0, prev_ref[...].astype(jnp.float32))
    nxt = jnp.where(pos == tiles_per_seq - 1, 0.0, next_ref[...].astype(jnp.float32))
    a = a_ref[...].astype(jnp.float32)
    n_slab = D // LANES
    for s in range(n_slab):
        cols = slice(s * LANES, (s + 1) * LANES)
        win_ref[s, 0:HALO, :] = prev[:, cols]
        win_ref[s, HALO:HALO + tm, :] = a[:, cols]
        win_ref[s, HALO + tm:, :] = nxt[:, cols]
    blk = PHASES * SUBLANES
    for s in range(n_slab):
        cols = slice(s * LANES, (s + 1) * LANES)
        taps = [jnp.broadcast_to(wd_ref[k:k + 1, cols], (SUBLANES, LANES)) for k in range(CONV_W)]

        def block(rb, carry, s=s, taps=taps):
            base = pl.multiple_of(rb * (CONV_UNROLL * blk), CONV_UNROLL * blk)
            for q in range(CONV_UNROLL * PHASES):
                row = (q // PHASES) * blk + q % PHASES
                acc = None
                for k in range(CONV_W):
                    off = row + k + HALO - CONV_PAD
                    term = win_ref[s, pl.ds(base + off, SUBLANES, stride=PHASES), :] * taps[k]
                    acc = term if acc is None else acc + term
                conv_ref[s, pl.ds(base + row, SUBLANES, stride=PHASES), :] = acc
            return carry

        jax.lax.fori_loop(0, tm // (CONV_UNROLL * blk), block, 0)
    conv = jnp.concatenate([conv_ref[s] for s in range(n_slab)], axis=1)
    y = _layer_norm(conv + bd_ref[...], lg_ref[...], lb_ref[...])
    y = (y * jax.nn.sigmoid(y)).astype(jnp.bfloat16)
    h1 = x_ref[...] + gate_ref[0] * (_dot(y, w2_ref[...]) + b2_ref[...])
    o_ref[...] = _mlp_tail(h1, *mlp_refs)


def _conv_mix(h, mods, a, w_dw, b_dw, ln_g, ln_b, w_pw2, b_pw2, norm_g, w1, w2, layer, rows_per_batch, tm):
    m = h.shape[0]
    ms = functools.partial(_mod_spec, layer, tm=tm, rows_per_batch=rows_per_batch)
    hb = tm // HALO
    last = m // HALO - 1
    vec = pl.BlockSpec((1, D), lambda i: (0, 0))
    return pl.pallas_call(
        functools.partial(_conv_kernel, tiles_per_seq=rows_per_batch // tm),
        out_shape=jax.ShapeDtypeStruct((m, D), jnp.float32),
        grid=(m // tm,),
        in_specs=[
            pl.BlockSpec((tm, D), lambda i: (i, 0)),
            ms(2),
            pl.BlockSpec((tm, D), lambda i: (i, 0)),
            pl.BlockSpec((HALO, D), lambda i: (jnp.maximum(i * hb - 1, 0), 0)),
            pl.BlockSpec((HALO, D), lambda i: (jnp.minimum((i + 1) * hb, last), 0)),
            pl.BlockSpec((CONV_W, D), lambda i: (0, 0)),
            vec, vec, vec,
            _resident((D, D), layer // N_MIXERS),
            vec,
        ] + _mlp_specs(layer, ms),
        out_specs=pl.BlockSpec((tm, D), lambda i: (i, 0)),
        scratch_shapes=[pltpu.VMEM((D // LANES, tm + 2 * HALO, LANES), jnp.float32),
                        pltpu.VMEM((D // LANES, tm, LANES), jnp.float32)],
        compiler_params=_cparams(("parallel",)),
        name="conv_mix_mlp",
    )(h, mods, a, a, a, w_dw, b_dw, ln_g, ln_b, w_pw2, b_pw2, norm_g, mods, mods, mods, w1, w2)


def kernel(x, c, ctx, c_ctx, ada_w, ada_b, norm1_g, norm2_g, a_w_in, a_b_in, a_ln_g, a_ln_b, a_w_s, a_b_s, a_w_out, b_w_qkv, b_w_o, b_q_g, b_k_g, b_rpb, c_w_pw1, c_b_pw1, c_w_dw, c_b_dw, c_ln_g, c_ln_b, c_w_pw2, c_b_pw2, mlp_w1, mlp_w2):
    bsz, seq, _ = x.shape
    ctx_len = ctx.shape[1]
    bf = jnp.bfloat16
    assert bsz <= CTX_ROW and seq % (2 * NA_R * GRID_W) == 0 and seq // GRID_W >= NA_BAND

    cond = jnp.zeros((MOD_ROWS, D), jnp.float32).at[:bsz].set(c).at[CTX_ROW].set(c_ctx)
    mods = _ada_mods(cond, ada_w, ada_b)
    n1 = norm1_g.reshape(DEPTH, 1, D)
    n2 = norm2_g.reshape(DEPTH, 1, D)

    h = x.reshape(bsz * seq, D)
    hc = ctx.reshape(bsz * ctx_len, D)
    tm, tmc = 512, 256
    w1, w2 = mlp_w1.astype(bf), mlp_w2.astype(bf)
    w_in, w_out, w_s = a_w_in.astype(bf), a_w_out.astype(bf), a_w_s.astype(bf)
    w_qkv, w_o = b_w_qkv.astype(bf), b_w_o.astype(bf)
    w_pw1, w_pw2 = c_w_pw1.astype(bf), c_w_pw2.astype(bf)

    for l in range(DEPTH):
        kind, j = l % N_MIXERS, l // N_MIXERS
        if kind == 0:
            b_in = a_b_in[j].reshape(1, -1)
            ln_g, ln_b = a_ln_g[j].reshape(1, -1), a_ln_b[j].reshape(1, -1)
            bs_full = jnp.repeat(a_b_s[j].T, SG_GW, axis=1)
            uv = _sgu_in(h, mods, n1, w_in, b_in, ln_g, ln_b, l, seq, tm)
            h = _sgu_out(h, mods, uv, w_s, bs_full, w_out, n2, w1, w2, l, seq, tm)
            if l < LAST_CTX_READER:
                uvc = _sgu_in(hc, mods, n1, w_in, b_in, ln_g, ln_b, l, None, tmc)
                hc = _sgu_out(hc, mods, uvc, w_s, bs_full, w_out, n2, w1, w2, l, None, tmc)
        elif kind == 1:
            gains = jnp.concatenate([jnp.tile(b_q_g[j], NA_HEADS) * (NA_DH ** -0.5 * LOG2E),
                                     jnp.tile(b_k_g[j], NA_HEADS),
                                     jnp.ones((D,), jnp.float32)]).reshape(1, 3 * D)
            qkv = _qkv(h, mods, n1, w_qkv, gains, l, seq, tm, 0)
            kvc = _qkv(hc, mods, n1, w_qkv, gains, l, None, tmc, 1)
            o = _na_attention(qkv.reshape(bsz, seq, 3 * D), kvc.reshape(bsz, ctx_len, 2 * D),
                              _na_bias_table(b_rpb[j]), bsz, seq, ctx_len)
            h = _proj_res(h, mods, o.reshape(bsz * seq, D), w_o, n2, w1, w2, l, seq, tm)
        else:
            a = _glu(h, mods, n1, w_pw1, c_b_pw1[j].reshape(1, -1), l, seq, tm)
            h = _conv_mix(h, mods, a, c_w_dw[j], c_b_dw[j].reshape(1, -1), c_ln_g[j].reshape(1, -1),
                          c_ln_b[j].reshape(1, -1), w_pw2, c_b_pw2[j].reshape(1, -1),
                          n2, w1, w2, l, seq, 256)
    return h.reshape(bsz, seq, D)
```

```python
import functools

import jax
import jax.numpy as jnp
import numpy as np
from jax.experimental import pallas as pl
from jax.experimental.pallas import tpu as pltpu

D = 1024
DEPTH = 4
GRID_W = 64
N_MIXERS = 3
CHUNK = 128
SG_GROUPS = 8
SG_HALF = 2 * D
SG_GW = SG_HALF // SG_GROUPS
SG_IN_CHUNK = 1024
NA_HEADS = 16
NA_DH = D // NA_HEADS
NA_KH = 8
NA_KW = 16
CONV_W = 31
CONV_PAD = CONV_W // 2
D_FF = 4 * D
MLP_TF = 1024
EPS = 1e-6
LAST_CTX_READER = max(range(1, DEPTH, N_MIXERS))

MOD_ROWS = 16
CTX_ROW = 8
NEG = -1e30
LOG2E = float(np.log2(np.e))
VMEM_LIMIT = 56 * 1024 * 1024

NA_R = 4
NA_BAND = 12
HALO = 16
SUBLANES, LANES = 8, 128
PHASES = 4
CONV_UNROLL = 4


def _cparams(sem):
    return pltpu.CompilerParams(dimension_semantics=sem, vmem_limit_bytes=VMEM_LIMIT)


def _norm_mod(x, g, sh, sc):
    ms = jnp.mean(x * x, axis=-1, keepdims=True)
    return x * jax.lax.rsqrt(ms + EPS) * (g * (1.0 + sc)) + sh


def _gelu_tanh(x):
    c = -2.0 * np.sqrt(2.0 / np.pi) * np.log2(np.e)
    return x / (1.0 + jnp.exp2(x * (c + (0.044715 * c) * (x * x))))


def _layer_norm(x, g, b):
    mu = jnp.mean(x, axis=-1, keepdims=True)
    xc = x - mu
    var = jnp.mean(xc * xc, axis=-1, keepdims=True)
    return xc * jax.lax.rsqrt(var + EPS) * g + b


def _dot(a, b):
    return jnp.dot(a, b, preferred_element_type=jnp.float32)


def _ada_kernel(c_ref, w_ref, b_ref, o_ref):
    cond = c_ref[...]
    act = (cond * jax.nn.sigmoid(cond)).astype(jnp.bfloat16)
    o_ref[0] = _dot(act, w_ref[0].astype(jnp.bfloat16)) + b_ref[0]


def _ada_mods(cond, ada_w, ada_b):
    tn = 1536
    out = pl.pallas_call(
        _ada_kernel,
        out_shape=jax.ShapeDtypeStruct((DEPTH, MOD_ROWS, 6 * D), jnp.float32),
        grid=(DEPTH, 6 * D // tn),
        in_specs=[
            pl.BlockSpec((MOD_ROWS, D), lambda l, n: (0, 0)),
            pl.BlockSpec((1, D, tn), lambda l, n: (l, 0, n)),
            pl.BlockSpec((1, 1, tn), lambda l, n: (l, 0, n)),
        ],
        out_specs=pl.BlockSpec((1, MOD_ROWS, tn), lambda l, n: (l, 0, n)),
        compiler_params=_cparams(("arbitrary", "arbitrary")),
        name="ada_mods",
    )(cond, ada_w, ada_b.reshape(DEPTH, 1, 6 * D))
    return out.reshape(DEPTH * MOD_ROWS, 1, 6 * D)


def _mod_spec(layer, chunk, tm, rows_per_batch):
    if rows_per_batch is None:
        return pl.BlockSpec((1, 1, D), lambda i, *_: (layer * MOD_ROWS + CTX_ROW, 0, chunk))
    return pl.BlockSpec(
        (1, 1, D), lambda i, *_: (layer * MOD_ROWS + (i * tm) // rows_per_batch, 0, chunk))


def _mlp_tail(h1, g_ref, sh_ref, sc_ref, gate_ref, w1_ref, w2_ref):
    hn = _norm_mod(h1, g_ref[0], sh_ref[0], sc_ref[0]).astype(jnp.bfloat16)
    acc = None
    for f in range(D_FF // MLP_TF):
        a = jnp.maximum(_dot(hn, w1_ref[:, f * MLP_TF:(f + 1) * MLP_TF]), 0.0)
        part = _dot((a * a).astype(jnp.bfloat16), w2_ref[f * MLP_TF:(f + 1) * MLP_TF, :])
        acc = part if acc is None else acc + part
    return h1 + gate_ref[0] * acc


def _resident(shape, index=None):
    if index is None:
        return pl.BlockSpec(shape, lambda i: (0,) * len(shape), pipeline_mode=pl.Buffered(1))
    return pl.BlockSpec((None,) + tuple(shape), lambda i: (index,) + (0,) * len(shape),
                        pipeline_mode=pl.Buffered(1))


def _mlp_specs(layer, ms):
    return [pl.BlockSpec((1, 1, D), lambda i: (layer, 0, 0)), ms(3), ms(4), ms(5),
            _resident((D, D_FF), layer), _resident((D_FF, D), layer)]


def _sgu_in_kernel(x_ref, g_ref, sh_ref, sc_ref, w_ref, b_ref, lg_ref, lb_ref, o_ref):
    hn = _norm_mod(x_ref[...], g_ref[0], sh_ref[0], sc_ref[0]).astype(jnp.bfloat16)
    tn = SG_IN_CHUNK

    def act(c):
        return _gelu_tanh(_dot(hn, w_ref[:, c * tn:(c + 1) * tn]) + b_ref[:, c * tn:(c + 1) * tn])

    vs = [act(c) for c in range(SG_HALF // tn, 2 * SG_HALF // tn)]
    mu = sum(jnp.sum(v, axis=-1, keepdims=True) for v in vs) * (1.0 / SG_HALF)
    vs = [v - mu for v in vs]
    var = sum(jnp.sum(v * v, axis=-1, keepdims=True) for v in vs) * (1.0 / SG_HALF)
    rstd = jax.lax.rsqrt(var + EPS)
    for c, v in enumerate(vs):
        cols = slice(c * tn, (c + 1) * tn)
        o_ref[:, SG_HALF + c * tn:SG_HALF + (c + 1) * tn] = (
            v * rstd * lg_ref[:, cols] + lb_ref[:, cols]).astype(jnp.bfloat16)
    for c in range(SG_HALF // tn):
        o_ref[:, c * tn:(c + 1) * tn] = act(c).astype(jnp.bfloat16)


def _sgu_in(h, mods, norm_g, w_in, b_in, ln_g, ln_b, layer, rows_per_batch, tm):
    m = h.shape[0]
    ms = functools.partial(_mod_spec, layer, tm=tm, rows_per_batch=rows_per_batch)
    return pl.pallas_call(
        _sgu_in_kernel,
        out_shape=jax.ShapeDtypeStruct((m, 2 * SG_HALF), jnp.bfloat16),
        grid=(m // tm,),
        in_specs=[
            pl.BlockSpec((tm, D), lambda i: (i, 0)),
            pl.BlockSpec((1, 1, D), lambda i: (layer, 0, 0)),
            ms(0), ms(1),
            _resident((D, 2 * SG_HALF), layer // N_MIXERS),
            _resident((1, 2 * SG_HALF)),
            _resident((1, SG_HALF)),
            _resident((1, SG_HALF)),
        ],
        out_specs=pl.BlockSpec((tm, 2 * SG_HALF), lambda i: (i, 0)),
        compiler_params=_cparams(("parallel",)),
        name="sgu_in",
    )(h, norm_g, mods, mods, w_in, b_in, ln_g, ln_b)


def _sgu_out_kernel(x_ref, gate_ref, u_ref, v_ref, ws_ref, bs_ref, wo_ref, *rest):
    *mlp_refs, o_ref, gated_ref = rest
    tm = x_ref.shape[0]
    for ci in range(tm // CHUNK):
        rows = slice(ci * CHUNK, (ci + 1) * CHUNK)
        for g in range(SG_GROUPS):
            cols = slice(g * SG_GW, (g + 1) * SG_GW)
            vs = _dot(ws_ref[g], v_ref[rows, cols]) + bs_ref[:, cols]
            gated_ref[rows, cols] = (u_ref[rows, cols].astype(jnp.float32) * vs).astype(jnp.bfloat16)
    h1 = x_ref[...] + gate_ref[0] * _dot(gated_ref[...], wo_ref[...])
    o_ref[...] = _mlp_tail(h1, *mlp_refs)


def _sgu_out(h, mods, uv, w_s, bs_full, w_out, norm_g, w1, w2, layer, rows_per_batch, tm):
    m = h.shape[0]
    ms = functools.partial(_mod_spec, layer, tm=tm, rows_per_batch=rows_per_batch)
    return pl.pallas_call(
        _sgu_out_kernel,
        out_shape=jax.ShapeDtypeStruct((m, D), jnp.float32),
        grid=(m // tm,),
        in_specs=[
            pl.BlockSpec((tm, D), lambda i: (i, 0)),
            ms(2),
            pl.BlockSpec((tm, SG_HALF), lambda i: (i, 0)),
            pl.BlockSpec((tm, SG_HALF), lambda i: (i, 1)),
            _resident((SG_GROUPS, CHUNK, CHUNK), layer // N_MIXERS),
            _resident((CHUNK, SG_HALF)),
            _resident((SG_HALF, D), layer // N_MIXERS),
        ] + _mlp_specs(layer, ms),
        out_specs=pl.BlockSpec((tm, D), lambda i: (i, 0)),
        scratch_shapes=[pltpu.VMEM((tm, SG_HALF), jnp.bfloat16)],
        compiler_params=_cparams(("parallel",)),
        name="sgu_out_mlp",
    )(h, mods, uv, uv, w_s, bs_full, w_out, norm_g, mods, mods, mods, w1, w2)


def _qkv_kernel(x_ref, g_ref, sh_ref, sc_ref, w_ref, hg_ref, o_ref, *, n_first):
    hn = _norm_mod(x_ref[...], g_ref[0], sh_ref[0], sc_ref[0]).astype(jnp.bfloat16)
    lane = jax.lax.broadcasted_iota(jnp.int32, (1, 2 * NA_DH), 1)
    first = lane < NA_DH
    for n in range(n_first, 3):
        y = _dot(hn, w_ref[:, n * D:(n + 1) * D])
        out0 = (n - n_first) * D
        if n == 2:
            o_ref[:, out0:out0 + D] = y.astype(jnp.bfloat16)
            continue
        for j in range(D // (2 * NA_DH)):
            t = y[:, j * 2 * NA_DH:(j + 1) * 2 * NA_DH]
            t2 = t * t
            s_all = jnp.sum(t2, axis=-1, keepdims=True)
            s_first = jnp.sum(jnp.where(first, t2, 0.0), axis=-1, keepdims=True)
            ms = jnp.where(first, s_first, s_all - s_first) * (1.0 / NA_DH)
            gain = hg_ref[:, n * D + j * 2 * NA_DH:n * D + (j + 1) * 2 * NA_DH]
            o_ref[:, out0 + j * 2 * NA_DH:out0 + (j + 1) * 2 * NA_DH] = (
                t * jax.lax.rsqrt(ms + EPS) * gain).astype(jnp.bfloat16)


def _qkv(h, mods, norm_g, w, head_gains, layer, rows_per_batch, tm, n_first):
    m = h.shape[0]
    nd = (3 - n_first) * D
    ms = functools.partial(_mod_spec, layer, tm=tm, rows_per_batch=rows_per_batch)
    return pl.pallas_call(
        functools.partial(_qkv_kernel, n_first=n_first),
        out_shape=jax.ShapeDtypeStruct((m, nd), jnp.bfloat16),
        grid=(m // tm,),
        in_specs=[
            pl.BlockSpec((tm, D), lambda i: (i, 0)),
            pl.BlockSpec((1, 1, D), lambda i: (layer, 0, 0)),
            ms(0), ms(1),
            _resident((D, 3 * D), layer // N_MIXERS),
            _resident((1, 3 * D)),
        ],
        out_specs=pl.BlockSpec((tm, nd), lambda i: (i, 0)),
        compiler_params=_cparams(("parallel",)),
        name="na_qkv",
    )(h, norm_g, mods, mods, w, head_gains)


def _na_band_start(rb, rows):
    return np.clip(rb * NA_R - NA_KH // 2, 0, rows - NA_BAND)


NA_MASKED = 2 * NA_KH - 1


def _na_pair_codes():
    off = range(2 * NA_KH - 1)
    return ([(NA_MASKED, NA_MASKED)] + [(e, e + 1) for e in off[:-1]]
            + [(e, NA_MASKED) for e in off] + [(NA_MASKED, e) for e in off])


def _na_bias_index(rows):
    codes = {pair: n for n, pair in enumerate(_na_pair_codes())}
    idx = np.zeros((rows // NA_R, NA_R, NA_BAND // 2), np.int32)
    for rb in range(rows // NA_R):
        kb = _na_band_start(rb, rows)
        for i in range(NA_R):
            r = rb * NA_R + i
            rs = np.clip(r - NA_KH // 2, 0, rows - NA_KH)
            rel = [kb + j - r + (NA_KH - 1) if rs <= kb + j < rs + NA_KH else NA_MASKED for j in range(NA_BAND)]
            for jp in range(NA_BAND // 2):
                idx[rb, i, jp] = codes[(rel[2 * jp], rel[2 * jp + 1])]
    return idx.reshape(-1)


def _na_kernel(idx_ref, q_ref, k_ref, v_ref, kc_ref, vc_ref, bias_ref, o_ref, s0_ref, s1_ref, p0_ref, p1_ref):
    g = pl.program_id(2)
    last = pl.num_programs(2) - 1
    tq = NA_R * GRID_W
    s_refs, p_refs = (s0_ref, s1_ref), (p0_ref, p1_ref)

    def scores(j):
        s_refs[j][...] = _na_scores(2 * g + j, q_ref[0, j * tq:(j + 1) * tq, :], k_ref, kc_ref[0])

    def probs(j):
        p_refs[1 - j][...] = _na_probs(2 * g + j - 1, idx_ref, s_refs[1 - j][...], bias_ref)

    def values(j):
        o_ref[0, j * tq:(j + 1) * tq, :] = _na_values(2 * g + j - 2, p_refs[j][...], v_ref, vc_ref[0])

    @pl.when(g == 0)
    def _():
        scores(0)
        scores(1)
        probs(1)

    @pl.when(jnp.logical_and(g > 0, g < last))
    def _():
        for j in range(2):
            scores(j)
            probs(j)
            values(j)

    @pl.when(g == last)
    def _():
        probs(0)
        values(0)
        values(1)


def _na_band(rb, ref, tail):
    rows = ref.shape[1] // GRID_W
    kb = jnp.clip(rb * NA_R - NA_KH // 2, 0, rows - NA_BAND)
    start = pl.multiple_of(kb * GRID_W, GRID_W)
    return jnp.concatenate([ref[0, pl.ds(start, NA_BAND * GRID_W), :], tail], axis=0)


def _na_head_lanes():
    first = jax.lax.broadcasted_iota(jnp.int32, (1, 2 * NA_DH), 1) < NA_DH
    return first, jnp.logical_not(first)


def _na_scores(rb, q, k_ref, kc):
    qm = jnp.concatenate([jnp.where(keep, q, jnp.zeros_like(q)) for keep in _na_head_lanes()], axis=0)
    return jax.lax.dot_general(qm, _na_band(rb, k_ref, kc), (((1,), (1,)), ((), ())),
                               preferred_element_type=jnp.float32)


def _na_probs(rb, idx_ref, s, bias_ref):
    tq = s.shape[0] // 2
    n_loc = NA_BAND * GRID_W
    out = []
    for hh in range(2):
        bias_rows = []
        for i in range(NA_R):
            base = (rb * NA_R + i) * (NA_BAND // 2)
            pieces = [bias_ref[hh, idx_ref[base + jp]] for jp in range(NA_BAND // 2)]
            bias_rows.append(jnp.concatenate(pieces, axis=1))
        s_loc = s[hh * tq:(hh + 1) * tq, :n_loc] + jnp.concatenate(bias_rows, axis=0)
        s_ctx = s[hh * tq:(hh + 1) * tq, n_loc:]
        mx = jnp.maximum(jnp.max(s_loc, axis=-1, keepdims=True), jnp.max(s_ctx, axis=-1, keepdims=True))
        out.append(jnp.concatenate([jnp.exp2(s_loc - mx), jnp.exp2(s_ctx - mx)], axis=1).astype(jnp.bfloat16))
    return jnp.concatenate(out, axis=0)


def _na_values(rb, p, v_ref, vc):
    tq = p.shape[0] // 2
    first, second = _na_head_lanes()
    vcat = _na_band(rb, v_ref, vc)
    one = jnp.ones((), jnp.bfloat16)
    outs = []
    for hh, keep in enumerate((first, second)):
        o = _dot(p[hh * tq:(hh + 1) * tq], jnp.where(keep, vcat, one))
        outs.append(o / pltpu.roll(o, NA_DH, axis=1))
    return jnp.where(first, outs[0], outs[1]).astype(jnp.bfloat16)


def _na_bias_table(rpb):
    cols = np.arange(GRID_W)
    cs = np.clip(cols - NA_KW // 2, 0, GRID_W - NA_KW)
    rel = cols[None, :] - cols[:, None] + (NA_KW - 1)
    ok = (cols[None, :] >= cs[:, None]) & (cols[None, :] < cs[:, None] + NA_KW)
    onehot = (ok[None] & (rel[None] == np.arange(2 * NA_KW - 1)[:, None, None])).astype(np.float32)
    dense = jnp.einsum('het,tqk->heqk', rpb, onehot, precision=jax.lax.Precision.HIGHEST)
    dense = jnp.where(ok[None, None], dense * LOG2E, NEG)
    dense = jnp.concatenate([dense, jnp.full((NA_HEADS, 1, GRID_W, GRID_W), NEG, jnp.float32)], axis=1)
    return jnp.stack([jnp.concatenate([dense[:, left], dense[:, right]], axis=-1)
                      for left, right in _na_pair_codes()], axis=1)


def _na_attention(qkv, kvc, bias, bsz, seq, ctx_len):
    hp = NA_HEADS // 2
    bw = 2 * NA_DH
    tq = 2 * NA_R * GRID_W
    steps = seq // tq
    n_keys = NA_BAND * GRID_W + ctx_len
    grid_spec = pltpu.PrefetchScalarGridSpec(
        num_scalar_prefetch=1,
        grid=(hp, bsz, steps + 1),
        in_specs=[
            pl.BlockSpec((1, tq, bw), lambda h, b, r, idx: (b, jnp.minimum(r, steps - 1), h)),
            pl.BlockSpec((1, seq, bw), lambda h, b, r, idx: (b, 0, hp + h)),
            pl.BlockSpec((1, seq, bw), lambda h, b, r, idx: (b, 0, 2 * hp + h)),
            pl.BlockSpec((1, ctx_len, bw), lambda h, b, r, idx: (b, 0, h)),
            pl.BlockSpec((1, ctx_len, bw), lambda h, b, r, idx: (b, 0, hp + h)),
            pl.BlockSpec((2,) + bias.shape[1:], lambda h, b, r, idx: (h, 0, 0, 0)),
        ],
        out_specs=pl.BlockSpec((1, tq, bw), lambda h, b, r, idx: (b, jnp.maximum(r - 1, 0), h)),
        scratch_shapes=[pltpu.VMEM((2 * NA_R * GRID_W, n_keys), dt)
                        for dt in (jnp.float32, jnp.float32, jnp.bfloat16, jnp.bfloat16)],
    )
    return pl.pallas_call(
        _na_kernel,
        out_shape=jax.ShapeDtypeStruct((bsz, seq, D), jnp.bfloat16),
        grid_spec=grid_spec,
        compiler_params=_cparams(("parallel", "parallel", "arbitrary")),
        name="na_attention",
    )(jnp.asarray(_na_bias_index(seq // GRID_W)), qkv, qkv, qkv, kvc, kvc, bias)


def _proj_res_kernel(x_ref, gate_ref, a_ref, w_ref, *rest):
    *mlp_refs, o_ref = rest
    h1 = x_ref[...] + gate_ref[0] * _dot(a_ref[...], w_ref[...])
    o_ref[...] = _mlp_tail(h1, *mlp_refs)


def _proj_res(h, mods, a, w, norm_g, w1, w2, layer, rows_per_batch, tm):
    m = h.shape[0]
    ms = functools.partial(_mod_spec, layer, tm=tm, rows_per_batch=rows_per_batch)
    return pl.pallas_call(
        _proj_res_kernel,
        out_shape=jax.ShapeDtypeStruct((m, D), jnp.float32),
        grid=(m // tm,),
        in_specs=[
            pl.BlockSpec((tm, D), lambda i: (i, 0)),
            ms(2),
            pl.BlockSpec((tm, D), lambda i: (i, 0)),
            _resident((D, D), layer // N_MIXERS),
        ] + _mlp_specs(layer, ms),
        out_specs=pl.BlockSpec((tm, D), lambda i: (i, 0)),
        compiler_params=_cparams(("parallel",)),
        name="na_out_proj_mlp",
    )(h, mods, a, w, norm_g, mods, mods, mods, w1, w2)


def _glu_kernel(x_ref, g_ref, sh_ref, sc_ref, w_ref, b_ref, o_ref):
    hn = _norm_mod(x_ref[...], g_ref[0], sh_ref[0], sc_ref[0]).astype(jnp.bfloat16)
    y = _dot(hn, w_ref[...]) + b_ref[...]
    o_ref[...] = (y[:, :D] * jax.nn.sigmoid(y[:, D:])).astype(jnp.bfloat16)


def _glu(h, mods, norm_g, w, b, layer, rows_per_batch, tm):
    m = h.shape[0]
    ms = functools.partial(_mod_spec, layer, tm=tm, rows_per_batch=rows_per_batch)
    return pl.pallas_call(
        _glu_kernel,
        out_shape=jax.ShapeDtypeStruct((m, D), jnp.bfloat16),
        grid=(m // tm,),
        in_specs=[
            pl.BlockSpec((tm, D), lambda i: (i, 0)),
            pl.BlockSpec((1, 1, D), lambda i: (layer, 0, 0)),
            ms(0), ms(1),
            _resident((D, 2 * D), layer // N_MIXERS),
            _resident((1, 2 * D)),
        ],
        out_specs=pl.BlockSpec((tm, D), lambda i: (i, 0)),
        compiler_params=_cparams(("parallel",)),
        name="conv_glu",
    )(h, norm_g, mods, mods, w, b)


def _conv_kernel(x_ref, gate_ref, a_ref, prev_ref, next_ref, wd_ref, bd_ref, lg_ref, lb_ref,
                 w2_ref, b2_ref, *rest, tiles_per_seq):
    *mlp_refs, o_ref, win_ref, conv_ref = rest
    tm = x_ref.shape[0]
    i = pl.program_id(0)
    pos = i % tiles_per_seq
    prev = jnp.where(pos == 0, 0.0, prev_ref[...].astype(jnp.float32))
    nxt = jnp.where(pos == tiles_per_seq - 1, 0.0, next_ref[...].astype(jnp.float32))
    a = a_ref[...].astype(jnp.float32)
    n_slab = D // LANES
    for s in range(n_slab):
        cols = slice(s * LANES, (s + 1) * LANES)
        win_ref[s, 0:HALO, :] = prev[:, cols]
        win_ref[s, HALO:HALO + tm, :] = a[:, cols]
        win_ref[s, HALO + tm:, :] = nxt[:, cols]
    blk = PHASES * SUBLANES
    for s in range(n_slab):
        cols = slice(s * LANES, (s + 1) * LANES)
        taps = [jnp.broadcast_to(wd_ref[k:k + 1, cols], (SUBLANES, LANES)) for k in range(CONV_W)]

        def block(rb, carry, s=s, taps=taps):
            base = pl.multiple_of(rb * (CONV_UNROLL * blk), CONV_UNROLL * blk)
            for q in range(CONV_UNROLL * PHASES):
                row = (q // PHASES) * blk + q % PHASES
                acc = None
                for k in range(CONV_W):
                    off = row + k + HALO - CONV_PAD
                    term = win_ref[s, pl.ds(base + off, SUBLANES, stride=PHASES), :] * taps[k]
                    acc = term if acc is None else acc + term
                conv_ref[s, pl.ds(base + row, SUBLANES, stride=PHASES), :] = acc
            return carry

        jax.lax.fori_loop(0, tm // (CONV_UNROLL * blk), block, 0)
    conv = jnp.concatenate([conv_ref[s] for s in range(n_slab)], axis=1)
    y = _layer_norm(conv + bd_ref[...], lg_ref[...], lb_ref[...])
    y = (y * jax.nn.sigmoid(y)).astype(jnp.bfloat16)
    h1 = x_ref[...] + gate_ref[0] * (_dot(y, w2_ref[...]) + b2_ref[...])
    o_ref[...] = _mlp_tail(h1, *mlp_refs)


def _conv_mix(h, mods, a, w_dw, b_dw, ln_g, ln_b, w_pw2, b_pw2, norm_g, w1, w2, layer, rows_per_batch, tm):
    m = h.shape[0]
    ms = functools.partial(_mod_spec, layer, tm=tm, rows_per_batch=rows_per_batch)
    hb = tm // HALO
    last = m // HALO - 1
    vec = pl.BlockSpec((1, D), lambda i: (0, 0))
    return pl.pallas_call(
        functools.partial(_conv_kernel, tiles_per_seq=rows_per_batch // tm),
        out_shape=jax.ShapeDtypeStruct((m, D), jnp.float32),
        grid=(m // tm,),
        in_specs=[
            pl.BlockSpec((tm, D), lambda i: (i, 0)),
            ms(2),
            pl.BlockSpec((tm, D), lambda i: (i, 0)),
            pl.BlockSpec((HALO, D), lambda i: (jnp.maximum(i * hb - 1, 0), 0)),
            pl.BlockSpec((HALO, D), lambda i: (jnp.minimum((i + 1) * hb, last), 0)),
            pl.BlockSpec((CONV_W, D), lambda i: (0, 0)),
            vec, vec, vec,
            _resident((D, D), layer // N_MIXERS),
            vec,
        ] + _mlp_specs(layer, ms),
        out_specs=pl.BlockSpec((tm, D), lambda i: (i, 0)),
        scratch_shapes=[pltpu.VMEM((D // LANES, tm + 2 * HALO, LANES), jnp.float32),
                        pltpu.VMEM((D // LANES, tm, LANES), jnp.float32)],
        compiler_params=_cparams(("parallel",)),
        name="conv_mix_mlp",
    )(h, mods, a, a, a, w_dw, b_dw, ln_g, ln_b, w_pw2, b_pw2, norm_g, mods, mods, mods, w1, w2)


def kernel(x, c, ctx, c_ctx, ada_w, ada_b, norm1_g, norm2_g, a_w_in, a_b_in, a_ln_g, a_ln_b, a_w_s, a_b_s, a_w_out, b_w_qkv, b_w_o, b_q_g, b_k_g, b_rpb, c_w_pw1, c_b_pw1, c_w_dw, c_b_dw, c_ln_g, c_ln_b, c_w_pw2, c_b_pw2, mlp_w1, mlp_w2):
    bsz, seq, _ = x.shape
    ctx_len = ctx.shape[1]
    bf = jnp.bfloat16
    assert bsz <= CTX_ROW and seq % (2 * NA_R * GRID_W) == 0 and seq // GRID_W >= NA_BAND

    cond = jnp.zeros((MOD_ROWS, D), jnp.float32).at[:bsz].set(c).at[CTX_ROW].set(c_ctx)
    mods = _ada_mods(cond, ada_w, ada_b)
    n1 = norm1_g.reshape(DEPTH, 1, D)
    n2 = norm2_g.reshape(DEPTH, 1, D)

    h = x.reshape(bsz * seq, D)
    hc = ctx.reshape(bsz * ctx_len, D)
    tm, tmc = 512, 256
    w1, w2 = mlp_w1.astype(bf), mlp_w2.astype(bf)
    w_in, w_out, w_s = a_w_in.astype(bf), a_w_out.astype(bf), a_w_s.astype(bf)
    w_qkv, w_o = b_w_qkv.astype(bf), b_w_o.astype(bf)
    w_pw1, w_pw2 = c_w_pw1.astype(bf), c_w_pw2.astype(bf)

    for l in range(DEPTH):
        kind, j = l % N_MIXERS, l // N_MIXERS
        if kind == 0:
            b_in = a_b_in[j].reshape(1, -1)
            ln_g, ln_b = a_ln_g[j].reshape(1, -1), a_ln_b[j].reshape(1, -1)
            bs_full = jnp.repeat(a_b_s[j].T, SG_GW, axis=1)
            uv = _sgu_in(h, mods, n1, w_in, b_in, ln_g, ln_b, l, seq, tm)
            h = _sgu_out(h, mods, uv, w_s, bs_full, w_out, n2, w1, w2, l, seq, tm)
            if l < LAST_CTX_READER:
                uvc = _sgu_in(hc, mods, n1, w_in, b_in, ln_g, ln_b, l, None, tmc)
                hc = _sgu_out(hc, mods, uvc, w_s, bs_full, w_out, n2, w1, w2, l, None, tmc)
        elif kind == 1:
            gains = jnp.concatenate([jnp.tile(b_q_g[j], NA_HEADS) * (NA_DH ** -0.5 * LOG2E),
                                     jnp.tile(b_k_g[j], NA_HEADS),
                                     jnp.ones((D,), jnp.float32)]).reshape(1, 3 * D)
            qkv = _qkv(h, mods, n1, w_qkv, gains, l, seq, tm, 0)
            kvc = _qkv(hc, mods, n1, w_qkv, gains, l, None, tmc, 1)
            o = _na_attention(qkv.reshape(bsz, seq, 3 * D), kvc.reshape(bsz, ctx_len, 2 * D),
                              _na_bias_table(b_rpb[j]), bsz, seq, ctx_len)
            h = _proj_res(h, mods, o.reshape(bsz * seq, D), w_o, n2, w1, w2, l, seq, tm)
        else:
            a = _glu(h, mods, n1, w_pw1, c_b_pw1[j].reshape(1, -1), l, seq, tm)
            h = _conv_mix(h, mods, a, c_w_dw[j], c_b_dw[j].reshape(1, -1), c_ln_g[j].reshape(1, -1),
                          c_ln_b[j].reshape(1, -1), w_pw2, c_b_pw2[j].reshape(1, -1),
                          n2, w1, w2, l, seq, tm)
    return h.reshape(bsz, seq, D)
```

```python
import functools

import jax
import jax.numpy as jnp
import numpy as np
from jax.experimental import pallas as pl
from jax.experimental.pallas import tpu as pltpu

D = 1024
DEPTH = 4
GRID_W = 64
N_MIXERS = 3
CHUNK = 128
SG_GROUPS = 8
SG_HALF = 2 * D
SG_GW = SG_HALF // SG_GROUPS
SG_IN_CHUNK = 1024
NA_HEADS = 16
NA_DH = D // NA_HEADS
NA_KH = 8
NA_KW = 16
CONV_W = 31
CONV_PAD = CONV_W // 2
D_FF = 4 * D
MLP_TF = 1024
EPS = 1e-6
LAST_CTX_READER = max(range(1, DEPTH, N_MIXERS))

MOD_ROWS = 16
CTX_ROW = 8
NEG = -1e30
LOG2E = float(np.log2(np.e))
VMEM_LIMIT = 56 * 1024 * 1024

NA_R = 4
NA_BAND = 12
HALO = 16
SUBLANES, LANES = 8, 128
PHASES = 4
CONV_UNROLL = 4


def _cparams(sem):
    return pltpu.CompilerParams(dimension_semantics=sem, vmem_limit_bytes=VMEM_LIMIT)


def _norm_mod(x, g, sh, sc):
    ms = jnp.mean(x * x, axis=-1, keepdims=True)
    return x * jax.lax.rsqrt(ms + EPS) * (g * (1.0 + sc)) + sh


def _gelu_tanh(x):
    c = -2.0 * np.sqrt(2.0 / np.pi) * np.log2(np.e)
    return x / (1.0 + jnp.exp2(x * (c + (0.044715 * c) * (x * x))))


def _layer_norm(x, g, b):
    mu = jnp.mean(x, axis=-1, keepdims=True)
    xc = x - mu
    var = jnp.mean(xc * xc, axis=-1, keepdims=True)
    return xc * jax.lax.rsqrt(var + EPS) * g + b


def _dot(a, b):
    return jnp.dot(a, b, preferred_element_type=jnp.float32)


def _ada_kernel(c_ref, w_ref, b_ref, o_ref):
    cond = c_ref[...]
    act = (cond * jax.nn.sigmoid(cond)).astype(jnp.bfloat16)
    o_ref[0] = _dot(act, w_ref[0].astype(jnp.bfloat16)) + b_ref[0]


def _ada_mods(cond, ada_w, ada_b):
    tn = 1536
    out = pl.pallas_call(
        _ada_kernel,
        out_shape=jax.ShapeDtypeStruct((DEPTH, MOD_ROWS, 6 * D), jnp.float32),
        grid=(DEPTH, 6 * D // tn),
        in_specs=[
            pl.BlockSpec((MOD_ROWS, D), lambda l, n: (0, 0)),
            pl.BlockSpec((1, D, tn), lambda l, n: (l, 0, n)),
            pl.BlockSpec((1, 1, tn), lambda l, n: (l, 0, n)),
        ],
        out_specs=pl.BlockSpec((1, MOD_ROWS, tn), lambda l, n: (l, 0, n)),
        compiler_params=_cparams(("arbitrary", "arbitrary")),
        name="ada_mods",
    )(cond, ada_w, ada_b.reshape(DEPTH, 1, 6 * D))
    return out.reshape(DEPTH * MOD_ROWS, 1, 6 * D)


def _mod_spec(layer, chunk, tm, rows_per_batch):
    if rows_per_batch is None:
        return pl.BlockSpec((1, 1, D), lambda i, *_: (layer * MOD_ROWS + CTX_ROW, 0, chunk))
    return pl.BlockSpec(
        (1, 1, D), lambda i, *_: (layer * MOD_ROWS + (i * tm) // rows_per_batch, 0, chunk))


def _mlp_tail(h1, g_ref, sh_ref, sc_ref, gate_ref, w1_ref, w2_ref):
    hn = _norm_mod(h1, g_ref[0], sh_ref[0], sc_ref[0]).astype(jnp.bfloat16)
    acc = None
    for f in range(D_FF // MLP_TF):
        a = jnp.maximum(_dot(hn, w1_ref[:, f * MLP_TF:(f + 1) * MLP_TF]), 0.0)
        part = _dot((a * a).astype(jnp.bfloat16), w2_ref[f * MLP_TF:(f + 1) * MLP_TF, :])
        acc = part if acc is None else acc + part
    return h1 + gate_ref[0] * acc


def _resident(shape, index=None):
    if index is None:
        return pl.BlockSpec(shape, lambda i: (0,) * len(shape), pipeline_mode=pl.Buffered(1))
    return pl.BlockSpec((None,) + tuple(shape), lambda i: (index,) + (0,) * len(shape),
                        pipeline_mode=pl.Buffered(1))


def _mlp_specs(layer, ms):
    return [pl.BlockSpec((1, 1, D), lambda i: (layer, 0, 0)), ms(3), ms(4), ms(5),
            _resident((D, D_FF), layer), _resident((D_FF, D), layer)]


def _sgu_in_kernel(x_ref, g_ref, sh_ref, sc_ref, w_ref, b_ref, lg_ref, lb_ref, o_ref):
    hn = _norm_mod(x_ref[...], g_ref[0], sh_ref[0], sc_ref[0]).astype(jnp.bfloat16)
    tn = SG_IN_CHUNK

    def act(c):
        return _gelu_tanh(_dot(hn, w_ref[:, c * tn:(c + 1) * tn]) + b_ref[:, c * tn:(c + 1) * tn])

    vs = [act(c) for c in range(SG_HALF // tn, 2 * SG_HALF // tn)]
    mu = sum(jnp.sum(v, axis=-1, keepdims=True) for v in vs) * (1.0 / SG_HALF)
    vs = [v - mu for v in vs]
    var = sum(jnp.sum(v * v, axis=-1, keepdims=True) for v in vs) * (1.0 / SG_HALF)
    rstd = jax.lax.rsqrt(var + EPS)
    for c, v in enumerate(vs):
        cols = slice(c * tn, (c + 1) * tn)
        o_ref[:, SG_HALF + c * tn:SG_HALF + (c + 1) * tn] = (
            v * rstd * lg_ref[:, cols] + lb_ref[:, cols]).astype(jnp.bfloat16)
    for c in range(SG_HALF // tn):
        o_ref[:, c * tn:(c + 1) * tn] = act(c).astype(jnp.bfloat16)


def _sgu_in(h, mods, norm_g, w_in, b_in, ln_g, ln_b, layer, rows_per_batch, tm):
    m = h.shape[0]
    ms = functools.partial(_mod_spec, layer, tm=tm, rows_per_batch=rows_per_batch)
    return pl.pallas_call(
        _sgu_in_kernel,
        out_shape=jax.ShapeDtypeStruct((m, 2 * SG_HALF), jnp.bfloat16),
        grid=(m // tm,),
        in_specs=[
            pl.BlockSpec((tm, D), lambda i: (i, 0)),
            pl.BlockSpec((1, 1, D), lambda i: (layer, 0, 0)),
            ms(0), ms(1),
            _resident((D, 2 * SG_HALF), layer // N_MIXERS),
            _resident((1, 2 * SG_HALF)),
            _resident((1, SG_HALF)),
            _resident((1, SG_HALF)),
        ],
        out_specs=pl.BlockSpec((tm, 2 * SG_HALF), lambda i: (i, 0)),
        compiler_params=_cparams(("parallel",)),
        name="sgu_in",
    )(h, norm_g, mods, mods, w_in, b_in, ln_g, ln_b)


def _sgu_out_kernel(x_ref, gate_ref, u_ref, v_ref, ws_ref, bs_ref, wo_ref, *rest):
    *mlp_refs, o_ref, gated_ref = rest
    tm = x_ref.shape[0]
    for ci in range(tm // CHUNK):
        rows = slice(ci * CHUNK, (ci + 1) * CHUNK)
        for g in range(SG_GROUPS):
            cols = slice(g * SG_GW, (g + 1) * SG_GW)
            vs = _dot(ws_ref[g], v_ref[rows, cols]) + bs_ref[:, cols]
            gated_ref[rows, cols] = (u_ref[rows, cols].astype(jnp.float32) * vs).astype(jnp.bfloat16)
    h1 = x_ref[...] + gate_ref[0] * _dot(gated_ref[...], wo_ref[...])
    o_ref[...] = _mlp_tail(h1, *mlp_refs)


def _sgu_out(h, mods, uv, w_s, bs_full, w_out, norm_g, w1, w2, layer, rows_per_batch, tm):
    m = h.shape[0]
    ms = functools.partial(_mod_spec, layer, tm=tm, rows_per_batch=rows_per_batch)
    return pl.pallas_call(
        _sgu_out_kernel,
        out_shape=jax.ShapeDtypeStruct((m, D), jnp.float32),
        grid=(m // tm,),
        in_specs=[
            pl.BlockSpec((tm, D), lambda i: (i, 0)),
            ms(2),
            pl.BlockSpec((tm, SG_HALF), lambda i: (i, 0)),
            pl.BlockSpec((tm, SG_HALF), lambda i: (i, 1)),
            _resident((SG_GROUPS, CHUNK, CHUNK), layer // N_MIXERS),
            _resident((CHUNK, SG_HALF)),
            _resident((SG_HALF, D), layer // N_MIXERS),
        ] + _mlp_specs(layer, ms),
        out_specs=pl.BlockSpec((tm, D), lambda i: (i, 0)),
        scratch_shapes=[pltpu.VMEM((tm, SG_HALF), jnp.bfloat16)],
        compiler_params=_cparams(("parallel",)),
        name="sgu_out_mlp",
    )(h, mods, uv, uv, w_s, bs_full, w_out, norm_g, mods, mods, mods, w1, w2)


def _qkv_kernel(x_ref, g_ref, sh_ref, sc_ref, w_ref, hg_ref, o_ref, *, n_first):
    hn = _norm_mod(x_ref[...], g_ref[0], sh_ref[0], sc_ref[0]).astype(jnp.bfloat16)
    lane = jax.lax.broadcasted_iota(jnp.int32, (1, 2 * NA_DH), 1)
    first = lane < NA_DH
    for n in range(n_first, 3):
        y = _dot(hn, w_ref[:, n * D:(n + 1) * D])
        out0 = (n - n_first) * D
        if n == 2:
            o_ref[:, out0:out0 + D] = y.astype(jnp.bfloat16)
            continue
        for j in range(D // (2 * NA_DH)):
            t = y[:, j * 2 * NA_DH:(j + 1) * 2 * NA_DH]
            t2 = t * t
            s_all = jnp.sum(t2, axis=-1, keepdims=True)
            s_first = jnp.sum(jnp.where(first, t2, 0.0), axis=-1, keepdims=True)
            ms = jnp.where(first, s_first, s_all - s_first) * (1.0 / NA_DH)
            gain = hg_ref[:, n * D + j * 2 * NA_DH:n * D + (j + 1) * 2 * NA_DH]
            o_ref[:, out0 + j * 2 * NA_DH:out0 + (j + 1) * 2 * NA_DH] = (
                t * jax.lax.rsqrt(ms + EPS) * gain).astype(jnp.bfloat16)


def _qkv(h, mods, norm_g, w, head_gains, layer, rows_per_batch, tm, n_first):
    m = h.shape[0]
    nd = (3 - n_first) * D
    ms = functools.partial(_mod_spec, layer, tm=tm, rows_per_batch=rows_per_batch)
    return pl.pallas_call(
        functools.partial(_qkv_kernel, n_first=n_first),
        out_shape=jax.ShapeDtypeStruct((m, nd), jnp.bfloat16),
        grid=(m // tm,),
        in_specs=[
            pl.BlockSpec((tm, D), lambda i: (i, 0)),
            pl.BlockSpec((1, 1, D), lambda i: (layer, 0, 0)),
            ms(0), ms(1),
            _resident((D, 3 * D), layer // N_MIXERS),
            _resident((1, 3 * D)),
        ],
        out_specs=pl.BlockSpec((tm, nd), lambda i: (i, 0)),
        compiler_params=_cparams(("parallel",)),
        name="na_qkv",
    )(h, norm_g, mods, mods, w, head_gains)


def _na_band_start(rb, rows):
    return np.clip(rb * NA_R - NA_KH // 2, 0, rows - NA_BAND)


NA_MASKED = 2 * NA_KH - 1


def _na_pair_codes():
    off = range(2 * NA_KH - 1)
    return ([(NA_MASKED, NA_MASKED)] + [(e, e + 1) for e in off[:-1]]
            + [(e, NA_MASKED) for e in off] + [(NA_MASKED, e) for e in off])


def _na_bias_index(rows):
    codes = {pair: n for n, pair in enumerate(_na_pair_codes())}
    idx = np.zeros((rows // NA_R, NA_R, NA_BAND // 2), np.int32)
    for rb in range(rows // NA_R):
        kb = _na_band_start(rb, rows)
        for i in range(NA_R):
            r = rb * NA_R + i
            rs = np.clip(r - NA_KH // 2, 0, rows - NA_KH)
            rel = [kb + j - r + (NA_KH - 1) if rs <= kb + j < rs + NA_KH else NA_MASKED for j in range(NA_BAND)]
            for jp in range(NA_BAND // 2):
                idx[rb, i, jp] = codes[(rel[2 * jp], rel[2 * jp + 1])]
    return idx.reshape(-1)


def _na_kernel(idx_ref, q_ref, k_ref, v_ref, kc_ref, vc_ref, bias_ref, o_ref, s0_ref, s1_ref, p0_ref, p1_ref):
    g = pl.program_id(2)
    last = pl.num_programs(2) - 1
    tq = NA_R * GRID_W
    s_refs, p_refs = (s0_ref, s1_ref), (p0_ref, p1_ref)

    def scores(j):
        s_refs[j][...] = _na_scores(2 * g + j, q_ref[0, j * tq:(j + 1) * tq, :], k_ref, kc_ref[0])

    def probs(j):
        p_refs[1 - j][...] = _na_probs(2 * g + j - 1, idx_ref, s_refs[1 - j][...], bias_ref)

    def values(j):
        o_ref[0, j * tq:(j + 1) * tq, :] = _na_values(2 * g + j - 2, p_refs[j][...], v_ref, vc_ref[0])

    @pl.when(g == 0)
    def _():
        scores(0)
        scores(1)
        probs(1)

    @pl.when(jnp.logical_and(g > 0, g < last))
    def _():
        for j in range(2):
            scores(j)
            probs(j)
            values(j)

    @pl.when(g == last)
    def _():
        probs(0)
        values(0)
        values(1)


def _na_band(rb, ref, tail):
    rows = ref.shape[1] // GRID_W
    kb = jnp.clip(rb * NA_R - NA_KH // 2, 0, rows - NA_BAND)
    start = pl.multiple_of(kb * GRID_W, GRID_W)
    return jnp.concatenate([ref[0, pl.ds(start, NA_BAND * GRID_W), :], tail], axis=0)


def _na_head_lanes():
    first = jax.lax.broadcasted_iota(jnp.int32, (1, 2 * NA_DH), 1) < NA_DH
    return first, jnp.logical_not(first)


def _na_scores(rb, q, k_ref, kc):
    qm = jnp.concatenate([jnp.where(keep, q, jnp.zeros_like(q)) for keep in _na_head_lanes()], axis=0)
    return jax.lax.dot_general(qm, _na_band(rb, k_ref, kc), (((1,), (1,)), ((), ())),
                               preferred_element_type=jnp.float32)


def _na_probs(rb, idx_ref, s, bias_ref):
    tq = s.shape[0] // 2
    n_loc = NA_BAND * GRID_W
    out = []
    for hh in range(2):
        bias_rows = []
        for i in range(NA_R):
            base = (rb * NA_R + i) * (NA_BAND // 2)
            pieces = [bias_ref[hh, idx_ref[base + jp]] for jp in range(NA_BAND // 2)]
            bias_rows.append(jnp.concatenate(pieces, axis=1))
        s_loc = s[hh * tq:(hh + 1) * tq, :n_loc] + jnp.concatenate(bias_rows, axis=0)
        s_ctx = s[hh * tq:(hh + 1) * tq, n_loc:]
        mx = jnp.maximum(jnp.max(s_loc, axis=-1, keepdims=True), jnp.max(s_ctx, axis=-1, keepdims=True))
        out.append(jnp.concatenate([jnp.exp2(s_loc - mx), jnp.exp2(s_ctx - mx)], axis=1).astype(jnp.bfloat16))
    return jnp.concatenate(out, axis=0)


def _na_values(rb, p, v_ref, vc):
    tq = p.shape[0] // 2
    first, second = _na_head_lanes()
    vcat = _na_band(rb, v_ref, vc)
    one = jnp.ones((), jnp.bfloat16)
    outs = []
    for hh, keep in enumerate((first, second)):
        o = _dot(p[hh * tq:(hh + 1) * tq], jnp.where(keep, vcat, one))
        outs.append(o / pltpu.roll(o, NA_DH, axis=1))
    return jnp.where(first, outs[0], outs[1]).astype(jnp.bfloat16)


def _na_bias_table(rpb):
    cols = np.arange(GRID_W)
    cs = np.clip(cols - NA_KW // 2, 0, GRID_W - NA_KW)
    rel = cols[None, :] - cols[:, None] + (NA_KW - 1)
    ok = (cols[None, :] >= cs[:, None]) & (cols[None, :] < cs[:, None] + NA_KW)
    onehot = (ok[None] & (rel[None] == np.arange(2 * NA_KW - 1)[:, None, None])).astype(np.float32)
    dense = jnp.einsum('het,tqk->heqk', rpb, onehot, precision=jax.lax.Precision.HIGHEST)
    dense = jnp.where(ok[None, None], dense * LOG2E, NEG)
    dense = jnp.concatenate([dense, jnp.full((NA_HEADS, 1, GRID_W, GRID_W), NEG, jnp.float32)], axis=1)
    return jnp.stack([jnp.concatenate([dense[:, left], dense[:, right]], axis=-1)
                      for left, right in _na_pair_codes()], axis=1)


def _na_attention(qkv, kvc, bias, bsz, seq, ctx_len):
    hp = NA_HEADS // 2
    bw = 2 * NA_DH
    tq = 2 * NA_R * GRID_W
    steps = seq // tq
    n_keys = NA_BAND * GRID_W + ctx_len
    grid_spec = pltpu.PrefetchScalarGridSpec(
        num_scalar_prefetch=1,
        grid=(hp, bsz, steps + 1),
        in_specs=[
            pl.BlockSpec((1, tq, bw), lambda h, b, r, idx: (b, jnp.minimum(r, steps - 1), h)),
            pl.BlockSpec((1, seq, bw), lambda h, b, r, idx: (b, 0, hp + h)),
            pl.BlockSpec((1, seq, bw), lambda h, b, r, idx: (b, 0, 2 * hp + h)),
            pl.BlockSpec((1, ctx_len, bw), lambda h, b, r, idx: (b, 0, h)),
            pl.BlockSpec((1, ctx_len, bw), lambda h, b, r, idx: (b, 0, hp + h)),
            pl.BlockSpec((2,) + bias.shape[1:], lambda h, b, r, idx: (h, 0, 0, 0)),
        ],
        out_specs=pl.BlockSpec((1, tq, bw), lambda h, b, r, idx: (b, jnp.maximum(r - 1, 0), h)),
        scratch_shapes=[pltpu.VMEM((2 * NA_R * GRID_W, n_keys), dt)
                        for dt in (jnp.float32, jnp.float32, jnp.bfloat16, jnp.bfloat16)],
    )
    return pl.pallas_call(
        _na_kernel,
        out_shape=jax.ShapeDtypeStruct((bsz, seq, D), jnp.bfloat16),
        grid_spec=grid_spec,
        compiler_params=_cparams(("parallel", "parallel", "arbitrary")),
        name="na_attention",
    )(jnp.asarray(_na_bias_index(seq // GRID_W)), qkv, qkv, qkv, kvc, kvc, bias)


def _proj_res_kernel(x_ref, gate_ref, a_ref, w_ref, *rest):
    *mlp_refs, o_ref = rest
    h1 = x_ref[...] + gate_ref[0] * _dot(a_ref[...], w_ref[...])
    o_ref[...] = _mlp_tail(h1, *mlp_refs)


def _proj_res(h, mods, a, w, norm_g, w1, w2, layer, rows_per_batch, tm):
    m = h.shape[0]
    ms = functools.partial(_mod_spec, layer, tm=tm, rows_per_batch=rows_per_batch)
    return pl.pallas_call(
        _proj_res_kernel,
        out_shape=jax.ShapeDtypeStruct((m, D), jnp.float32),
        grid=(m // tm,),
        in_specs=[
            pl.BlockSpec((tm, D), lambda i: (i, 0)),
            ms(2),
            pl.BlockSpec((tm, D), lambda i: (i, 0)),
            _resident((D, D), layer // N_MIXERS),
        ] + _mlp_specs(layer, ms),
        out_specs=pl.BlockSpec((tm, D), lambda i: (i, 0)),
        compiler_params=_cparams(("parallel",)),
        name="na_out_proj_mlp",
    )(h, mods, a, w, norm_g, mods, mods, mods, w1, w2)


def _glu_kernel(x_ref, g_ref, sh_ref, sc_ref, w_ref, b_ref, o_ref):
    hn = _norm_mod(x_ref[...], g_ref[0], sh_ref[0], sc_ref[0]).astype(jnp.bfloat16)
    y = _dot(hn, w_ref[...]) + b_ref[...]
    o_ref[...] = (y[:, :D] * jax.nn.sigmoid(y[:, D:])).astype(jnp.bfloat16)


def _glu(h, mods, norm_g, w, b, layer, rows_per_batch, tm):
    m = h.shape[0]
    ms = functools.partial(_mod_spec, layer, tm=tm, rows_per_batch=rows_per_batch)
    return pl.pallas_call(
        _glu_kernel,
        out_shape=jax.ShapeDtypeStruct((m, D), jnp.bfloat16),
        grid=(m // tm,),
        in_specs=[
            pl.BlockSpec((tm, D), lambda i: (i, 0)),
            pl.BlockSpec((1, 1, D), lambda i: (layer, 0, 0)),
            ms(0), ms(1),
            _resident((D, 2 * D), layer // N_MIXERS),
            _resident((1, 2 * D)),
        ],
        out_specs=pl.BlockSpec((tm, D), lambda i: (i, 0)),
        compiler_params=_cparams(("parallel",)),
        name="conv_glu",
    )(h, norm_g, mods, mods, w, b)


def _conv_kernel(x_ref, gate_ref, a_ref, prev_ref, next_ref, wd_ref, bd_ref, lg_ref, lb_ref,
                 w2_ref, b2_ref, *rest, tiles_per_seq):
    *mlp_refs, o_ref, win_ref, conv_ref = rest
    tm = x_ref.shape[0]
    i = pl.program_id(0)
    pos = i % tiles_per_seq
    prev = jnp.where(pos == 0, 0.0, prev_ref[...].astype(jnp.float32))
    nxt = jnp.where(pos == tiles_per_seq - 1, 0.0, next_ref[...].astype(jnp.float32))
    a = a_ref[...].astype(jnp.float32)
    n_slab = D // LANES
    for s in range(n_slab):
        cols = slice(s * LANES, (s + 1) * LANES)
        win_ref[s, 0:HALO, :] = prev[:, cols]
        win_ref[s, HALO:HALO + tm, :] = a[:, cols]
        win_ref[s, HALO + tm:, :] = nxt[:, cols]
    blk = PHASES * SUBLANES
    for s in range(n_slab):
        cols = slice(s * LANES, (s + 1) * LANES)
        taps = [jnp.broadcast_to(wd_ref[k:k + 1, cols], (SUBLANES, LANES)) for k in range(CONV_W)]

        def block(rb, carry, s=s, taps=taps):
            base = pl.multiple_of(rb * (CONV_UNROLL * blk), CONV_UNROLL * blk)
            for q in range(CONV_UNROLL * PHASES):
                row = (q // PHASES) * blk + q % PHASES
                acc = None
                for k in range(CONV_W):
                    off = row + k + HALO - CONV_PAD
                    term = win_ref[s, pl.ds(base + off, SUBLANES, stride=PHASES), :] * taps[k]
                    acc = term if acc is None else acc + term
                conv_ref[s, pl.ds(base + row, SUBLANES, stride=PHASES), :] = acc
            return carry

        jax.lax.fori_loop(0, tm // (CONV_UNROLL * blk), block, 0)
    conv = jnp.concatenate([conv_ref[s] for s in range(n_slab)], axis=1)
    y = _layer_norm(conv + bd_ref[...], lg_ref[...], lb_ref[...])
    y = (y * jax.nn.sigmoid(y)).astype(jnp.bfloat16)
    h1 = x_ref[...] + gate_ref[0] * (_dot(y, w2_ref[...]) + b2_ref[...])
    o_ref[...] = _mlp_tail(h1, *mlp_refs)


def _conv_mix(h, mods, a, w_dw, b_dw, ln_g, ln_b, w_pw2, b_pw2, norm_g, w1, w2, layer, rows_per_batch, tm):
    m = h.shape[0]
    ms = functools.partial(_mod_spec, layer, tm=tm, rows_per_batch=rows_per_batch)
    hb = tm // HALO
    last = m // HALO - 1
    vec = pl.BlockSpec((1, D), lambda i: (0, 0))
    return pl.pallas_call(
        functools.partial(_conv_kernel, tiles_per_seq=rows_per_batch // tm),
        out_shape=jax.ShapeDtypeStruct((m, D), jnp.float32),
        grid=(m // tm,),
        in_specs=[
            pl.BlockSpec((tm, D), lambda i: (i, 0)),
            ms(2),
            pl.BlockSpec((tm, D), lambda i: (i, 0)),
            pl.BlockSpec((HALO, D), lambda i: (jnp.maximum(i * hb - 1, 0), 0)),
            pl.BlockSpec((HALO, D), lambda i: (jnp.minimum((i + 1) * hb, last), 0)),
            pl.BlockSpec((CONV_W, D), lambda i: (0, 0)),
            vec, vec, vec,
            _resident((D, D), layer // N_MIXERS),
            vec,
        ] + _mlp_specs(layer, ms),
        out_specs=pl.BlockSpec((tm, D), lambda i: (i, 0)),
        scratch_shapes=[pltpu.VMEM((D // LANES, tm + 2 * HALO, LANES), jnp.float32),
                        pltpu.VMEM((D // LANES, tm, LANES), jnp.float32)],
        compiler_params=_cparams(("parallel",)),
        name="conv_mix_mlp",
    )(h, mods, a, a, a, w_dw, b_dw, ln_g, ln_b, w_pw2, b_pw2, norm_g, mods, mods, mods, w1, w2)


def kernel(x, c, ctx, c_ctx, ada_w, ada_b, norm1_g, norm2_g, a_w_in, a_b_in, a_ln_g, a_ln_b, a_w_s, a_b_s, a_w_out, b_w_qkv, b_w_o, b_q_g, b_k_g, b_rpb, c_w_pw1, c_b_pw1, c_w_dw, c_b_dw, c_ln_g, c_ln_b, c_w_pw2, c_b_pw2, mlp_w1, mlp_w2):
    bsz, seq, _ = x.shape
    ctx_len = ctx.shape[1]
    bf = jnp.bfloat16
    assert bsz <= CTX_ROW and seq % (2 * NA_R * GRID_W) == 0 and seq // GRID_W >= NA_BAND

    cond = jnp.zeros((MOD_ROWS, D), jnp.float32).at[:bsz].set(c).at[CTX_ROW].set(c_ctx)
    mods = _ada_mods(cond, ada_w, ada_b)
    n1 = norm1_g.reshape(DEPTH, 1, D)
    n2 = norm2_g.reshape(DEPTH, 1, D)

    h = x.reshape(bsz * seq, D)
    hc = ctx.reshape(bsz * ctx_len, D)
    tm, tmc = 512, 256
    w1, w2 = mlp_w1.astype(bf), mlp_w2.astype(bf)
    w_in, w_out, w_s = a_w_in.astype(bf), a_w_out.astype(bf), a_w_s.astype(bf)
    w_qkv, w_o = b_w_qkv.astype(bf), b_w_o.astype(bf)
    w_pw1, w_pw2 = c_w_pw1.astype(bf), c_w_pw2.astype(bf)

    for l in range(DEPTH):
        kind, j = l % N_MIXERS, l // N_MIXERS
        if kind == 0:
            b_in = a_b_in[j].reshape(1, -1)
            ln_g, ln_b = a_ln_g[j].reshape(1, -1), a_ln_b[j].reshape(1, -1)
            bs_full = jnp.repeat(a_b_s[j].T, SG_GW, axis=1)
            uv = _sgu_in(h, mods, n1, w_in, b_in, ln_g, ln_b, l, seq, tm)
            h = _sgu_out(h, mods, uv, w_s, bs_full, w_out, n2, w1, w2, l, seq, tm)
            if l < LAST_CTX_READER:
                uvc = _sgu_in(hc, mods, n1, w_in, b_in, ln_g, ln_b, l, None, tmc)
                hc = _sgu_out(hc, mods, uvc, w_s, bs_full, w_out, n2, w1, w2, l, None, tmc)
        elif kind == 1:
            gains = jnp.concatenate([jnp.tile(b_q_g[j], NA_HEADS) * (NA_DH ** -0.5 * LOG2E),
                                     jnp.tile(b_k_g[j], NA_HEADS),
                                     jnp.ones((D,), jnp.float32)]).reshape(1, 3 * D)
            qkv = _qkv(h, mods, n1, w_qkv, gains, l, seq, 2 * tm, 0)
            kvc = _qkv(hc, mods, n1, w_qkv, gains, l, None, tmc, 1)
            o = _na_attention(qkv.reshape(bsz, seq, 3 * D), kvc.reshape(bsz, ctx_len, 2 * D),
                              _na_bias_table(b_rpb[j]), bsz, seq, ctx_len)
            h = _proj_res(h, mods, o.reshape(bsz * seq, D), w_o, n2, w1, w2, l, seq, tm)
        else:
            a = _glu(h, mods, n1, w_pw1, c_b_pw1[j].reshape(1, -1), l, seq, 2 * tm)
            h = _conv_mix(h, mods, a, c_w_dw[j], c_b_dw[j].reshape(1, -1), c_ln_g[j].reshape(1, -1),
                          c_ln_b[j].reshape(1, -1), w_pw2, c_b_pw2[j].reshape(1, -1),
                          n2, w1, w2, l, seq, tm)
    return h.reshape(bsz, seq, D)
```
